```python
import math
import jax, jax.numpy as jnp
from jax import lax
import numpy as np

D_MODEL = 1024
BATCH = 4
SEQ = 4096
DEPTH = 4

DN_HEADS = 8
DN_HEAD_DIM = 128
DN_WIDTH = DN_HEADS * DN_HEAD_DIM
CONV_K = 4
CHUNK = 64
SSM_WIDTH = D_MODEL
SSM_GROUP = 16
SSM_GROUPS = SSM_WIDTH // SSM_GROUP
SSM_STATE = 64
DT_MIN = 0.001
DT_MAX = 0.1
D_IN = 3 * DN_WIDTH + DN_WIDTH + 2 * DN_HEADS + 2 * SSM_WIDTH + 2 * D_MODEL
EPS = 1e-6

kernel_name = "hybrid_deltanet_s5_gated_merge"


def _in_split_points():
    sizes = (3 * DN_WIDTH, DN_WIDTH, DN_HEADS, DN_HEADS, SSM_WIDTH, SSM_WIDTH, D_MODEL, D_MODEL)
    return [int(p) for p in np.cumsum(sizes)[:-1]]


def rms_norm(x, gain):
    xf = x.astype(jnp.float32)
    xf = xf * lax.rsqrt(jnp.mean(xf * xf, axis=-1, keepdims=True) + EPS)
    return (xf * gain.astype(jnp.float32)).astype(x.dtype)


def l2_normalize(x):
    xf = x.astype(jnp.float32)
    return xf * lax.rsqrt(jnp.sum(xf * xf, axis=-1, keepdims=True) + EPS)


def causal_depthwise_conv(x, w):
    return lax.conv_general_dilated(
        x, w[:, None, :].astype(x.dtype), window_strides=(1,), padding=[(CONV_K - 1, 0)],
        dimension_numbers=("NWC", "WIO", "NWC"), feature_group_count=x.shape[-1])


def chunked_gated_delta_rule(q, k, v, g, beta):
    b, s, h, d = q.shape
    n = s // CHUNK

    def chunks(t):
        t = jnp.moveaxis(t, 2, 1)
        return t.reshape((b, h, n, CHUNK) + t.shape[3:])

    q = chunks(q) * (d ** -0.5)
    k = chunks(k)
    v = chunks(v)
    g = jnp.cumsum(chunks(g), axis=-1)
    beta = chunks(beta)[..., None]
    idx = jnp.arange(CHUNK)
    causal = idx[:, None] >= idx[None, :]
    strict = idx[:, None] > idx[None, :]
    decay = jnp.exp(jnp.where(causal, g[..., :, None] - g[..., None, :], -jnp.inf))
    kb = k * beta
    l_mat = jnp.where(strict, jnp.einsum("bhncd,bhnjd->bhncj", kb, k) * decay, 0.0)
    rhs = jnp.concatenate([v * beta, kb * jnp.exp(g)[..., None]], axis=-1)
    sol = lax.linalg.triangular_solve(l_mat, rhs, left_side=True, lower=True, unit_diagonal=True)
    u_val, w = sol[..., :d], sol[..., d:]
    a_qk = jnp.einsum("bhncd,bhnjd->bhncj", q, k) * decay
    q_dec = q * jnp.exp(g)[..., None]
    g_last = g[..., -1]
    k_dec = k * jnp.exp(g_last[..., None] - g)[..., None]

    def step(state, xs):
        u_i, w_i, q_i, k_i, a_i, gl_i = xs
        v_new = u_i - jnp.einsum("bhcd,bhde->bhce", w_i, state)
        o_i = jnp.einsum("bhcd,bhde->bhce", q_i, state) + jnp.einsum("bhcj,bhje->bhce", a_i, v_new)
        state = state * jnp.exp(gl_i)[..., None, None] + jnp.einsum("bhcd,bhce->bhde", k_i, v_new)
        return state, o_i

    xs = tuple(jnp.moveaxis(t, 2, 0) for t in (u_val, w, q_dec, k_dec, a_qk, g_last))
    state0 = jnp.zeros((b, h, d, v.shape[-1]), jnp.float32)
    _, o = lax.scan(step, state0, xs)
    o = jnp.moveaxis(o, 0, 2).reshape(b, h, s, -1)
    return jnp.moveaxis(o, 1, 2)


def s5_ssm(u, a_re, a_im, log_dt, b_re, b_im, c_re, c_im, d_skip):
    bsz, s, _ = u.shape
    uf = u.astype(jnp.float32).reshape(bsz, s, SSM_GROUPS, SSM_GROUP)
    dt = jnp.exp(log_dt.astype(jnp.float32))[:, None]
    ar = a_re.astype(jnp.float32)
    ai = a_im.astype(jnp.float32)
    mag = jnp.exp(ar * dt)
    lr, li = mag * jnp.cos(ai * dt), mag * jnp.sin(ai * dt)
    den = ar * ar + ai * ai
    fr = ((lr - 1.0) * ar + li * ai) / den
    fi = (li * ar - (lr - 1.0) * ai) / den
    br, bi = b_re.astype(jnp.float32), b_im.astype(jnp.float32)
    bbr = fr[..., None] * br - fi[..., None] * bi
    bbi = fr[..., None] * bi + fi[..., None] * br
    xr = jnp.einsum("bsgc,gnc->bsgn", uf, bbr)
    xi = jnp.einsum("bsgc,gnc->bsgn", uf, bbi)
    lam_r = jnp.broadcast_to(lr, (s,) + lr.shape)[None]
    lam_i = jnp.broadcast_to(li, (s,) + li.shape)[None]

    def combine(e1, e2):
        a1r, a1i, b1r, b1i = e1
        a2r, a2i, b2r, b2i = e2
        return (a2r * a1r - a2i * a1i, a2r * a1i + a2i * a1r,
                a2r * b1r - a2i * b1i + b2r, a2r * b1i + a2i * b1r + b2i)

    _, _, hr, hi = lax.associative_scan(combine, (lam_r, lam_i, xr, xi), axis=1)
    y = (jnp.einsum("bsgn,gcn->bsgc", hr, c_re.astype(jnp.float32))
         - jnp.einsum("bsgn,gcn->bsgc", hi, c_im.astype(jnp.float32)))
    return y.reshape(bsz, s, SSM_WIDTH) + d_skip.astype(jnp.float32) * u.astype(jnp.float32)


def hybrid_layer(x, norm_pre, w_in, conv_w, a_log, dt_bias, head_norm,
                 ssm_a_re, ssm_a_im, ssm_log_dt, ssm_b_re, ssm_b_im, ssm_c_re, ssm_c_im, ssm_d,
                 w_glu, b_glu, w_out, norm_post):
    bsz, s, _ = x.shape
    h = rms_norm(x, norm_pre)
    proj = h @ w_in
    qkv, z_a, beta_logit, decay_logit, u, z_b, r_a, r_b = jnp.split(proj, _in_split_points(), axis=-1)

    qkv = jax.nn.silu(causal_depthwise_conv(qkv, conv_w))
    q, k, v = jnp.split(qkv, 3, axis=-1)
    heads = lambda t: t.reshape(bsz, s, DN_HEADS, DN_HEAD_DIM)
    q, k = l2_normalize(heads(q)), l2_normalize(heads(k))
    v = heads(v).astype(jnp.float32)
    beta = jax.nn.sigmoid(beta_logit.astype(jnp.float32))
    g = -jnp.exp(a_log.astype(jnp.float32)) * jax.nn.softplus(
        decay_logit.astype(jnp.float32) + dt_bias.astype(jnp.float32))
    o = chunked_gated_delta_rule(q, k, v, g, beta)
    y_a = rms_norm(o, head_norm).reshape(bsz, s, DN_WIDTH).astype(x.dtype) * jax.nn.silu(z_a)

    y = jax.nn.gelu(s5_ssm(u, ssm_a_re, ssm_a_im, ssm_log_dt, ssm_b_re, ssm_b_im,
                           ssm_c_re, ssm_c_im, ssm_d))
    y = y * jax.nn.sigmoid(y @ w_glu.astype(jnp.float32) + b_glu.astype(jnp.float32))
    y_b = y.astype(x.dtype) * jax.nn.silu(z_b)

    merged = jax.nn.sigmoid(r_a) * y_a + jax.nn.sigmoid(r_b) * y_b
    return x + rms_norm(merged @ w_out, norm_post)


def setup_inputs(seed: int = 0) -> dict:
    key = jax.random.key(seed)
    ks = jax.random.split(key, 20)
    f32 = jnp.float32
    nrm = lambda k, shape, scale: jax.random.normal(k, shape, f32) * scale
    gain = lambda k, shape: 1.0 + 0.02 * jax.random.normal(k, shape, f32)
    dt = jnp.exp(jax.random.uniform(ks[4], (DEPTH, DN_HEADS), f32, math.log(DT_MIN), math.log(DT_MAX)))
    return {
        "x": jax.random.normal(ks[0], (BATCH, SEQ, D_MODEL), f32),
        "norm_pre": gain(ks[1], (DEPTH, D_MODEL)),
        "w_in": nrm(ks[2], (DEPTH, D_MODEL, D_IN), D_MODEL ** -0.5),
        "conv_w": nrm(ks[3], (DEPTH, CONV_K, 3 * DN_WIDTH), CONV_K ** -0.5),
        "a_log": jnp.log(jax.random.uniform(ks[5], (DEPTH, DN_HEADS), f32, 1.0, 16.0)),
        "dt_bias": dt + jnp.log(-jnp.expm1(-dt)),
        "head_norm": gain(ks[6], (DEPTH, DN_HEAD_DIM)),
        "ssm_a_re": -0.5 + 0.01 * jax.random.normal(ks[7], (DEPTH, SSM_GROUPS, SSM_STATE), f32),
        "ssm_a_im": math.pi * jnp.arange(SSM_STATE, dtype=f32)
                    + 0.01 * jax.random.normal(ks[8], (DEPTH, SSM_GROUPS, SSM_STATE), f32),
        "ssm_log_dt": jax.random.uniform(ks[9], (DEPTH, SSM_GROUPS), f32, math.log(DT_MIN), math.log(DT_MAX)),
        "ssm_b_re": nrm(ks[10], (DEPTH, SSM_GROUPS, SSM_STATE, SSM_GROUP), (2 * SSM_GROUP) ** -0.5),
        "ssm_b_im": nrm(ks[11], (DEPTH, SSM_GROUPS, SSM_STATE, SSM_GROUP), (2 * SSM_GROUP) ** -0.5),
        "ssm_c_re": nrm(ks[12], (DEPTH, SSM_GROUPS, SSM_GROUP, SSM_STATE), (2 * SSM_STATE) ** -0.5),
        "ssm_c_im": nrm(ks[13], (DEPTH, SSM_GROUPS, SSM_GROUP, SSM_STATE), (2 * SSM_STATE) ** -0.5),
        "ssm_d": nrm(ks[14], (DEPTH, SSM_WIDTH), 1.0),
        "w_glu": nrm(ks[15], (DEPTH, SSM_WIDTH, SSM_WIDTH), SSM_WIDTH ** -0.5),
        "b_glu": nrm(ks[16], (DEPTH, SSM_WIDTH), 0.01),
        "w_out": nrm(ks[17], (DEPTH, D_MODEL, D_MODEL), D_MODEL ** -0.5),
        "norm_post": gain(ks[18], (DEPTH, D_MODEL)),
    }


def reference(x, norm_pre, w_in, conv_w, a_log, dt_bias, head_norm,
              ssm_a_re, ssm_a_im, ssm_log_dt, ssm_b_re, ssm_b_im, ssm_c_re, ssm_c_im, ssm_d,
              w_glu, b_glu, w_out, norm_post):
    for i in range(DEPTH):
        x = hybrid_layer(x, norm_pre[i], w_in[i], conv_w[i], a_log[i], dt_bias[i], head_norm[i],
                         ssm_a_re[i], ssm_a_im[i], ssm_log_dt[i], ssm_b_re[i], ssm_b_im[i],
                         ssm_c_re[i], ssm_c_im[i], ssm_d[i], w_glu[i], b_glu[i], w_out[i], norm_post[i])
    return x
```

```python
import functools

import jax
import jax.numpy as jnp
from jax import lax
from jax.experimental import pallas as pl
from jax.experimental.pallas import tpu as pltpu

D_MODEL = 1024
DN_HEADS = 8
DN_HEAD_DIM = 128
DN_WIDTH = DN_HEADS * DN_HEAD_DIM
CONV_K = 4
CHUNK = 64
SSM_GROUP = 16
SSM_GROUPS = 64
SSM_STATE = 64
EPS = 1e-6

LANES = 128
SUBLANES = 8
VMEM_LIMIT = 56 * 1024 * 1024

PROJ_W = 8 * D_MODEL + LANES
COL_ZA, COL_U, COL_ZB, COL_RA, COL_RB = 3, 4, 5, 6, 7
COL_BD = 8 * D_MODEL // LANES
DECAY_LANE0 = DN_HEADS

S5_BLOCKS = 4
S5_BLK_CH = D_MODEL // S5_BLOCKS
S5_BLK_ST = SSM_GROUPS * SSM_STATE // S5_BLOCKS
S5_SLAB = 256

F32 = jnp.float32
BF16 = jnp.bfloat16
HIGHEST = lax.Precision.HIGHEST


def _sigmoid(x):
    return 1.0 / (1.0 + jnp.exp(-x))


def _silu(x):
    return x * _sigmoid(x)


def _gelu_tanh(x):
    c = 0.7978845608028654
    return 0.5 * x * (1.0 + jnp.tanh(c * (x + 0.044715 * (x * x * x))))


def _softplus(x):
    return jnp.maximum(x, 0.0) + jnp.log(1.0 + jnp.exp(-jnp.abs(x)))


def _mm(a, b):
    return jnp.dot(a.astype(BF16), b.astype(BF16), preferred_element_type=F32)


def _mm_nt(a, b):
    return lax.dot_general(a.astype(BF16), b.astype(BF16), (((1,), (1,)), ((), ())),
                           preferred_element_type=F32)


def _mm_tn(a, b):
    return lax.dot_general(a.astype(BF16), b.astype(BF16), (((0,), (0,)), ((), ())),
                           preferred_element_type=F32)


def _mm_f32(a, b):
    return jnp.dot(a, b, precision=HIGHEST, preferred_element_type=F32)


def _mm_nt_f32(a, b):
    return lax.dot_general(a, b, (((1,), (1,)), ((), ())), precision=HIGHEST,
                           preferred_element_type=F32)


def _in_proj_kernel(x_ref, g_ref, w_ref, o_ref):
    x = x_ref[...]
    h = x * lax.rsqrt(jnp.mean(x * x, axis=-1, keepdims=True) + EPS) * g_ref[...]
    hb = h.astype(BF16)
    for a in range(0, PROJ_W, D_MODEL):
        b = min(a + D_MODEL, PROJ_W)
        o_ref[:, a:b] = jnp.dot(hb, w_ref[:, a:b], preferred_element_type=F32)


def _in_proj(x2, gain, w, tm):
    t = x2.shape[0]
    return pl.pallas_call(
        _in_proj_kernel,
        grid=(t // tm,),
        in_specs=[
            pl.BlockSpec((tm, D_MODEL), lambda i: (i, 0)),
            pl.BlockSpec((1, D_MODEL), lambda i: (0, 0)),
            pl.BlockSpec((D_MODEL, PROJ_W), lambda i: (0, 0), pipeline_mode=pl.Buffered(1)),
        ],
        out_specs=pl.BlockSpec((tm, PROJ_W), lambda i: (i, 0)),
        out_shape=jax.ShapeDtypeStruct((t, PROJ_W), F32),
        compiler_params=pltpu.CompilerParams(
            dimension_semantics=("arbitrary",), vmem_limit_bytes=VMEM_LIMIT),
        name="in_proj",
    )(x2, gain, w)


def _neumann_inverse(l_mat):
    n = l_mat.shape[0]
    row = lax.broadcasted_iota(jnp.int32, (n, n), 0)
    col = lax.broadcasted_iota(jnp.int32, (n, n), 1)
    eye = (row == col).astype(F32)
    m = -l_mat
    p = eye + m
    steps = (n - 1).bit_length() - 1
    for _ in range(steps):
        m = _mm_f32(m, m)
        p = p + _mm_f32(p, m)
    return p


def _conv_silu(xbuf, cw_ref, c, lo):
    win = xbuf[c * CHUNK:c * CHUNK + CHUNK + SUBLANES, lo:lo + LANES]
    w = cw_ref[:, lo:lo + LANES]
    acc = win[SUBLANES:] * w[CONV_K - 1:CONV_K]
    for j in range(CONV_K - 1):
        shifted = pltpu.roll(win, CONV_K - 1 - j, axis=0)[SUBLANES:]
        acc = acc + shifted * w[j:j + 1]
    return _silu(acc)


def _dn_kernel(q_ref, k_ref, v_ref, bd_ref, cwq_ref, cwk_ref, cwv_ref, gp_ref, hn_ref,
               o_ref, xq, xk, xv, state, *, ts, heads):
    hb = pl.program_id(1)
    t = pl.program_id(2)

    @pl.when(t == 0)
    def _():
        state[...] = jnp.zeros_like(state)
        zero_halo = jnp.zeros((SUBLANES, heads * LANES), F32)
        xq[0:SUBLANES, :] = zero_halo
        xk[0:SUBLANES, :] = zero_halo
        xv[0:SUBLANES, :] = zero_halo

    xq[SUBLANES:, :] = q_ref[...]
    xk[SUBLANES:, :] = k_ref[...]
    xv[SUBLANES:, :] = v_ref[...]

    bd = bd_ref[...]
    beta_all = _sigmoid(bd)
    g_all = -jnp.exp(gp_ref[0:1, :]) * _softplus(bd + gp_ref[1:2, :])
    head_gain = hn_ref[0:1, :]

    lane = lax.broadcasted_iota(jnp.int32, (CHUNK, LANES), 1)
    lane1 = lax.broadcasted_iota(jnp.int32, (1, LANES), 1)
    row = lax.broadcasted_iota(jnp.int32, (CHUNK, CHUNK), 0)
    col = lax.broadcasted_iota(jnp.int32, (CHUNK, CHUNK), 1)
    causal = row >= col
    strict = row > col
    tril = causal.astype(F32)
    scale = DN_HEAD_DIM ** -0.5

    s_list = [state[hl] for hl in range(heads)]
    for c in range(ts // CHUNK):
        rows = slice(c * CHUNK, (c + 1) * CHUNK)
        g_cum = _mm_f32(tril, g_all[rows, :])
        beta_c = beta_all[rows, :]
        for hl in range(heads):
            hg = hb * heads + hl
            lo = hl * LANES
            q = _conv_silu(xq, cwq_ref, c, lo)
            k = _conv_silu(xk, cwk_ref, c, lo)
            v = _conv_silu(xv, cwv_ref, c, lo)
            qn = q * (lax.rsqrt(jnp.sum(q * q, axis=-1, keepdims=True) + EPS) * scale)
            kn = k * lax.rsqrt(jnp.sum(k * k, axis=-1, keepdims=True) + EPS)

            beta = jnp.sum(jnp.where(lane == hg, beta_c, 0.0), axis=-1, keepdims=True)
            g_col = jnp.sum(jnp.where(lane == hg + DECAY_LANE0, g_cum, 0.0), axis=-1, keepdims=True)
            onehot = (lane == hg + DECAY_LANE0).astype(F32)
            g_row = _mm_nt_f32(onehot, g_cum)
            g_last = jnp.sum(jnp.where(lane1 == hg + DECAY_LANE0, g_cum[CHUNK - 1:CHUNK, :], 0.0),
                             axis=-1, keepdims=True)
            decay = jnp.exp(jnp.where(causal, g_col - g_row, -jnp.inf))
            e_g = jnp.exp(g_col)

            kb = kn * beta
            l_mat = jnp.where(strict, _mm_nt(kb, kn) * decay, 0.0)
            t_inv = _neumann_inverse(l_mat)
            rhs = jnp.concatenate([v * beta, kb * e_g], axis=-1)
            sol = _mm_f32(t_inv, rhs)
            u_val = sol[:, :DN_HEAD_DIM]
            w = sol[:, DN_HEAD_DIM:]
            a_qk = _mm_nt(qn, kn) * decay
            q_dec = qn * e_g
            k_dec = kn * jnp.exp(g_last - g_col)

            s = s_list[hl]
            v_new = u_val - _mm(w, s)
            o = _mm(q_dec, s) + _mm(a_qk, v_new)
            s_list[hl] = s * jnp.exp(g_last) + _mm_tn(k_dec, v_new)

            o = o * lax.rsqrt(jnp.mean(o * o, axis=-1, keepdims=True) + EPS) * head_gain
            o_ref[rows, lo:lo + LANES] = o

    for hl in range(heads):
        state[hl] = s_list[hl]
    xq[0:SUBLANES, :] = xq[ts:ts + SUBLANES, :]
    xk[0:SUBLANES, :] = xk[ts:ts + SUBLANES, :]
    xv[0:SUBLANES, :] = xv[ts:ts + SUBLANES, :]


def _deltanet(proj, conv_w8, gate_p, head_n, bsz, seq, ts, heads):
    nt = seq // ts
    wid = heads * LANES
    nqb = DN_WIDTH // wid
    tok = lambda b, h, t: b * nt + t
    return pl.pallas_call(
        functools.partial(_dn_kernel, ts=ts, heads=heads),
        grid=(bsz, DN_HEADS // heads, nt),
        in_specs=[
            pl.BlockSpec((ts, wid), lambda b, h, t: (tok(b, h, t), h)),
            pl.BlockSpec((ts, wid), lambda b, h, t: (tok(b, h, t), nqb + h)),
            pl.BlockSpec((ts, wid), lambda b, h, t: (tok(b, h, t), 2 * nqb + h)),
            pl.BlockSpec((ts, LANES), lambda b, h, t: (tok(b, h, t), COL_BD)),
            pl.BlockSpec((SUBLANES, wid), lambda b, h, t: (0, h)),
            pl.BlockSpec((SUBLANES, wid), lambda b, h, t: (0, nqb + h)),
            pl.BlockSpec((SUBLANES, wid), lambda b, h, t: (0, 2 * nqb + h)),
            pl.BlockSpec((SUBLANES, LANES), lambda b, h, t: (0, 0)),
            pl.BlockSpec((SUBLANES, LANES), lambda b, h, t: (0, 0)),
        ],
        out_specs=pl.BlockSpec((ts, wid), lambda b, h, t: (tok(b, h, t), h)),
        out_shape=jax.ShapeDtypeStruct((bsz * seq, DN_WIDTH), F32),
        scratch_shapes=[
            pltpu.VMEM((ts + SUBLANES, wid), F32),
            pltpu.VMEM((ts + SUBLANES, wid), F32),
            pltpu.VMEM((ts + SUBLANES, wid), F32),
            pltpu.VMEM((heads, DN_HEAD_DIM, DN_HEAD_DIM), F32),
        ],
        compiler_params=pltpu.CompilerParams(
            dimension_semantics=("arbitrary", "arbitrary", "arbitrary"), vmem_limit_bytes=VMEM_LIMIT),
        name="deltanet",
    )(proj, proj, proj, proj, conv_w8, conv_w8, conv_w8, gate_p, head_n)


def _s5_param_kernel(are, aim, ldt, arer, aimr, ldtr, brt, bit, cimt, lpr, lpi, bbr, bbi, ncim):
    ar = are[0]
    ai = aim[0]
    dt = jnp.exp(ldt[0])
    for k in range(SUBLANES):
        mag = jnp.exp(ar * dt * float(k + 1))
        ang = ai * dt * float(k + 1)
        lpr[0, k] = mag * jnp.cos(ang)
        lpi[0, k] = mag * jnp.sin(ang)
    ar = arer[0]
    ai = aimr[0]
    dt = jnp.exp(ldtr[0])
    mag = jnp.exp(ar * dt)
    lr = mag * jnp.cos(ai * dt)
    li = mag * jnp.sin(ai * dt)
    den = ar * ar + ai * ai
    fr = ((lr - 1.0) * ar + li * ai) / den
    fi = (li * ar - (lr - 1.0) * ai) / den
    br = brt[0]
    bi = bit[0]
    bbr[0] = fr * br - fi * bi
    bbi[0] = fr * bi + fi * br
    ncim[0] = -cimt[0]


def _s5_params(a_re, a_im, log_dt, b_re, b_im, c_im):
    nl = a_re.shape[0]
    g, n, c = SSM_GROUPS, SSM_STATE, SSM_GROUP
    ldt = jnp.broadcast_to(log_dt[..., None], (nl, g, n))
    rep = lambda a: jnp.repeat(a, c, axis=1)
    brt = jnp.swapaxes(b_re, 2, 3).reshape(nl, g * c, n)
    bit = jnp.swapaxes(b_im, 2, 3).reshape(nl, g * c, n)
    cimt = c_im.reshape(nl, g * c, n)
    small = pl.BlockSpec((1, g, n), lambda l: (l, 0, 0))
    big = pl.BlockSpec((1, g * c, n), lambda l: (l, 0, 0))
    powr = pl.BlockSpec((1, SUBLANES, g, n), lambda l: (l, 0, 0, 0))
    return pl.pallas_call(
        _s5_param_kernel,
        grid=(nl,),
        in_specs=[small, small, small, big, big, big, big, big, big],
        out_specs=[powr, powr, big, big, big],
        out_shape=[
            jax.ShapeDtypeStruct((nl, SUBLANES, g, n), F32),
            jax.ShapeDtypeStruct((nl, SUBLANES, g, n), F32),
            jax.ShapeDtypeStruct((nl, g * c, n), F32),
            jax.ShapeDtypeStruct((nl, g * c, n), F32),
            jax.ShapeDtypeStruct((nl, g * c, n), F32),
        ],
        name="s5_params",
    )(a_re, a_im, ldt, rep(a_re), rep(a_im), rep(ldt), brt, bit, cimt)


def _s5_assemble(lpr, lpi, bbr, bbi, c_re, ncim):
    gb = SSM_GROUPS // S5_BLOCKS
    eye = jnp.eye(gb, dtype=F32)

    def bblock(bb):
        b4 = bb.reshape(S5_BLOCKS, gb, SSM_GROUP, SSM_STATE)
        return jnp.einsum("jgcn,gh->jgchn", b4, eye).reshape(S5_BLOCKS, S5_BLK_CH, S5_BLK_ST)

    def cblock(cc):
        c4 = cc.reshape(S5_BLOCKS, gb, SSM_GROUP, SSM_STATE)
        return jnp.einsum("jgcn,gh->jgnhc", c4, eye).reshape(S5_BLOCKS, S5_BLK_ST, S5_BLK_CH)

    bblk = jnp.concatenate([bblock(bbr), bblock(bbi)], axis=-1).astype(BF16)
    cblk = jnp.concatenate([cblock(c_re.reshape(SSM_GROUPS * SSM_GROUP, SSM_STATE)), cblock(ncim)],
                           axis=1).astype(BF16)

    def table(lp):
        lp = lp.reshape(SUBLANES, S5_BLOCKS, S5_BLK_ST)
        r = jnp.arange(SUBLANES)[:, None, None]
        kinds = [jnp.where(r >= s, lp[s - 1][None], 0.0) for s in (1, 2, 4)] + [lp]
        return jnp.transpose(jnp.stack(kinds, axis=0), (2, 0, 1, 3))

    ptab = jnp.concatenate([table(lpr), table(lpi)], axis=-1)
    return bblk, cblk, ptab


def _s5_kernel(u_ref, bblk_ref, cblk_ref, p_ref, d_ref, y_ref, xs, hs, carry, *, ts):
    t = pl.program_id(1)

    @pl.when(t == 0)
    def _():
        carry[...] = jnp.zeros_like(carry)

    u = u_ref[...]
    ub = u.astype(BF16)
    for j in range(S5_BLOCKS):
        ch = slice(j * S5_BLK_CH, (j + 1) * S5_BLK_CH)
        xs[...] = jnp.dot(ub[:, ch], bblk_ref[j], preferred_element_type=F32)
        for s in range(S5_BLK_ST // S5_SLAB):
            re = slice(s * S5_SLAB, (s + 1) * S5_SLAB)
            im = slice(S5_BLK_ST + s * S5_SLAB, S5_BLK_ST + (s + 1) * S5_SLAB)
            steps = [(p_ref[j, i, :, re], p_ref[j, i, :, im], sh) for i, sh in enumerate((1, 2, 4))]
            lam_r = p_ref[j, 3, :, re]
            lam_i = p_ref[j, 3, :, im]

            def body(r, c, re=re, im=im, steps=steps, lam_r=lam_r, lam_i=lam_i):
                cr, ci = c
                r8 = pl.multiple_of(r * SUBLANES, SUBLANES)
                xr = xs[pl.ds(r8, SUBLANES), re]
                xi = xs[pl.ds(r8, SUBLANES), im]
                for pr, pi, sh in steps:
                    sr = pltpu.roll(xr, sh, axis=0)
                    si = pltpu.roll(xi, sh, axis=0)
                    xr, xi = xr + pr * sr - pi * si, xi + pr * si + pi * sr
                hr = xr + lam_r * cr - lam_i * ci
                hi = xi + lam_r * ci + lam_i * cr
                hs[pl.ds(r8, SUBLANES), re] = hr
                hs[pl.ds(r8, SUBLANES), im] = hi
                shape = (SUBLANES, S5_SLAB)
                return (jnp.broadcast_to(hr[SUBLANES - 1:, :], shape),
                        jnp.broadcast_to(hi[SUBLANES - 1:, :], shape))

            cr, ci = lax.fori_loop(0, ts // SUBLANES, body, (carry[j, :, re], carry[j, :, im]), unroll=2)
            carry[j, :, re] = cr
            carry[j, :, im] = ci
        y = jnp.dot(hs[...].astype(BF16), cblk_ref[j], preferred_element_type=F32)
        y_ref[:, ch] = y + d_ref[:, ch] * u[:, ch]


def _s5(proj, bblk, cblk, ptab, d_skip, bsz, seq, ts):
    nt = seq // ts
    const3 = lambda b, t: (0, 0, 0)
    return pl.pallas_call(
        functools.partial(_s5_kernel, ts=ts),
        grid=(bsz, nt),
        in_specs=[
            pl.BlockSpec((ts, D_MODEL), lambda b, t: (b * nt + t, COL_U)),
            pl.BlockSpec((S5_BLOCKS, S5_BLK_CH, 2 * S5_BLK_ST), const3),
            pl.BlockSpec((S5_BLOCKS, 2 * S5_BLK_ST, S5_BLK_CH), const3),
            pl.BlockSpec((S5_BLOCKS, 4, SUBLANES, 2 * S5_BLK_ST), lambda b, t: (0, 0, 0, 0)),
            pl.BlockSpec((1, D_MODEL), lambda b, t: (0, 0)),
        ],
        out_specs=pl.BlockSpec((ts, D_MODEL), lambda b, t: (b * nt + t, 0)),
        out_shape=jax.ShapeDtypeStruct((bsz * seq, D_MODEL), F32),
        scratch_shapes=[
            pltpu.VMEM((ts, 2 * S5_BLK_ST), F32),
            pltpu.VMEM((ts, 2 * S5_BLK_ST), F32),
            pltpu.VMEM((S5_BLOCKS, SUBLANES, 2 * S5_BLK_ST), F32),
        ],
        compiler_params=pltpu.CompilerParams(
            dimension_semantics=("arbitrary", "arbitrary"), vmem_limit_bytes=VMEM_LIMIT),
        name="s5",
    )(proj, bblk, cblk, ptab, d_skip)


def _merge_kernel(ya_ref, za_ref, ys_ref, zb_ref, ra_ref, rb_ref, x_ref, wglu_ref, bglu_ref,
                  wout_ref, np_ref, o_ref):
    y_a = ya_ref[...] * _silu(za_ref[...])
    y = _gelu_tanh(ys_ref[...])
    y = y * _sigmoid(jnp.dot(y.astype(BF16), wglu_ref[...], preferred_element_type=F32) + bglu_ref[...])
    y_b = y * _silu(zb_ref[...])
    merged = _sigmoid(ra_ref[...]) * y_a + _sigmoid(rb_ref[...]) * y_b
    out = jnp.dot(merged.astype(BF16), wout_ref[...], preferred_element_type=F32)
    out = out * lax.rsqrt(jnp.mean(out * out, axis=-1, keepdims=True) + EPS) * np_ref[...]
    o_ref[...] = x_ref[...] + out


def _merge(y_a, y_s, proj, x2, w_glu, b_glu, w_out, norm_post, tm):
    t = x2.shape[0]
    col = lambda c: pl.BlockSpec((tm, D_MODEL), lambda i: (i, c))
    vec = pl.BlockSpec((1, D_MODEL), lambda i: (0, 0))
    mat = pl.BlockSpec((D_MODEL, D_MODEL), lambda i: (0, 0))
    return pl.pallas_call(
        _merge_kernel,
        grid=(t // tm,),
        in_specs=[col(0), col(COL_ZA), col(0), col(COL_ZB), col(COL_RA), col(COL_RB), col(0),
                  mat, vec, mat, vec],
        out_specs=col(0),
        out_shape=jax.ShapeDtypeStruct((t, D_MODEL), F32),
        compiler_params=pltpu.CompilerParams(
            dimension_semantics=("arbitrary",), vmem_limit_bytes=VMEM_LIMIT),
        name="merge",
    )(y_a, proj, y_s, proj, proj, proj, x2, w_glu, b_glu, w_out, norm_post)


def _prep_w_in(w_in):
    o = 3 * DN_WIDTH
    qkv, za = w_in[..., :o], w_in[..., o:o + DN_WIDTH]
    o += DN_WIDTH
    bd = w_in[..., o:o + 2 * DN_HEADS]
    o += 2 * DN_HEADS
    u, zb, ra, rb = (w_in[..., o + i * D_MODEL:o + (i + 1) * D_MODEL] for i in range(4))
    pad = jnp.zeros(w_in.shape[:-1] + (LANES - 2 * DN_HEADS,), w_in.dtype)
    return jnp.concatenate([qkv, za, u, zb, ra, rb, bd, pad], axis=-1).astype(BF16)


def _pad_rows(a, rows=SUBLANES):
    return jnp.concatenate([a, jnp.zeros((rows - a.shape[0],) + a.shape[1:], a.dtype)], axis=0)


def _trunk(x, norm_pre, w_in, conv_w, a_log, dt_bias, head_norm, ssm_a_re, ssm_a_im, ssm_log_dt,
           ssm_b_re, ssm_b_im, ssm_c_re, ssm_c_im, ssm_d, w_glu, b_glu, w_out, norm_post,
           *, tm=256, ts_dn=256, dn_heads=2, ts_s5=256):
    bsz, seq, _ = x.shape
    depth = w_in.shape[0]
    x2 = x.reshape(bsz * seq, D_MODEL)
    w_in_r = _prep_w_in(w_in)
    w_glu_b = w_glu.astype(BF16)
    w_out_b = w_out.astype(BF16)
    lpr, lpi, bbr, bbi, ncim = _s5_params(ssm_a_re, ssm_a_im, ssm_log_dt, ssm_b_re, ssm_b_im, ssm_c_im)
    lane_pad = jnp.zeros((LANES - 2 * DN_HEADS,), F32)
    for i in range(depth):
        proj = _in_proj(x2, norm_pre[i][None], w_in_r[i], tm)
        gate_p = _pad_rows(jnp.stack([
            jnp.concatenate([jnp.zeros((DN_HEADS,), F32), a_log[i], lane_pad]),
            jnp.concatenate([jnp.zeros((DN_HEADS,), F32), dt_bias[i], lane_pad])]))
        y_a = _deltanet(proj, _pad_rows(conv_w[i]), gate_p, _pad_rows(head_norm[i][None]),
                        bsz, seq, ts_dn, dn_heads)
        bblk, cblk, ptab = _s5_assemble(lpr[i], lpi[i], bbr[i], bbi[i], ssm_c_re[i], ncim[i])
        y_s = _s5(proj, bblk, cblk, ptab, ssm_d[i][None], bsz, seq, ts_s5)
        x2 = _merge(y_a, y_s, proj, x2, w_glu_b[i], b_glu[i][None], w_out_b[i], norm_post[i][None], tm)
    return x2.reshape(bsz, seq, D_MODEL)


def kernel(x, norm_pre, w_in, conv_w, a_log, dt_bias, head_norm, ssm_a_re, ssm_a_im, ssm_log_dt,
           ssm_b_re, ssm_b_im, ssm_c_re, ssm_c_im, ssm_d, w_glu, b_glu, w_out, norm_post):
    return _trunk(x, norm_pre, w_in, conv_w, a_log, dt_bias, head_norm, ssm_a_re, ssm_a_im, ssm_log_dt,
                  ssm_b_re, ssm_b_im, ssm_c_re, ssm_c_im, ssm_d, w_glu, b_glu, w_out, norm_post)
```

```python
import functools

import jax
import jax.numpy as jnp
from jax import lax
from jax.experimental import pallas as pl
from jax.experimental.pallas import tpu as pltpu

D_MODEL = 1024
DN_HEADS = 8
DN_HEAD_DIM = 128
DN_WIDTH = DN_HEADS * DN_HEAD_DIM
CONV_K = 4
CHUNK = 64
SSM_GROUP = 16
SSM_GROUPS = 64
SSM_STATE = 64
EPS = 1e-6

LANES = 128
SUBLANES = 8
VMEM_LIMIT = 56 * 1024 * 1024

PROJ_W = 8 * D_MODEL + LANES
COL_ZA, COL_U, COL_ZB, COL_RA, COL_RB = 3, 4, 5, 6, 7
COL_BD = 8 * D_MODEL // LANES
DECAY_LANE0 = DN_HEADS

S5_BLOCKS = 4
S5_BLK_CH = D_MODEL // S5_BLOCKS
S5_BLK_ST = SSM_GROUPS * SSM_STATE // S5_BLOCKS
S5_SLAB = 256

F32 = jnp.float32
BF16 = jnp.bfloat16
HIGHEST = lax.Precision.HIGHEST


def _sigmoid(x):
    return 1.0 / (1.0 + jnp.exp(-x))


def _silu(x):
    return x * _sigmoid(x)


def _gelu_tanh(x):
    c = 0.7978845608028654
    return 0.5 * x * (1.0 + jnp.tanh(c * (x + 0.044715 * (x * x * x))))


def _softplus(x):
    return jnp.maximum(x, 0.0) + jnp.log(1.0 + jnp.exp(-jnp.abs(x)))


def _mm(a, b):
    return jnp.dot(a.astype(BF16), b.astype(BF16), preferred_element_type=F32)


def _mm_nt(a, b):
    return lax.dot_general(a.astype(BF16), b.astype(BF16), (((1,), (1,)), ((), ())),
                           preferred_element_type=F32)


def _mm_tn(a, b):
    return lax.dot_general(a.astype(BF16), b.astype(BF16), (((0,), (0,)), ((), ())),
                           preferred_element_type=F32)


def _mm_f32(a, b):
    return jnp.dot(a, b, precision=HIGHEST, preferred_element_type=F32)


def _mm_nt_f32(a, b):
    return lax.dot_general(a, b, (((1,), (1,)), ((), ())), precision=HIGHEST,
                           preferred_element_type=F32)


def _in_proj_kernel(x_ref, g_ref, w_ref, o_ref):
    x = x_ref[...]
    h = x * lax.rsqrt(jnp.mean(x * x, axis=-1, keepdims=True) + EPS) * g_ref[...]
    hb = h.astype(BF16)
    for a in range(0, PROJ_W, D_MODEL):
        b = min(a + D_MODEL, PROJ_W)
        o_ref[:, a:b] = jnp.dot(hb, w_ref[:, a:b], preferred_element_type=F32)


def _in_proj(x2, gain, w, tm):
    t = x2.shape[0]
    return pl.pallas_call(
        _in_proj_kernel,
        grid=(t // tm,),
        in_specs=[
            pl.BlockSpec((tm, D_MODEL), lambda i: (i, 0)),
            pl.BlockSpec((1, D_MODEL), lambda i: (0, 0)),
            pl.BlockSpec((D_MODEL, PROJ_W), lambda i: (0, 0), pipeline_mode=pl.Buffered(1)),
        ],
        out_specs=pl.BlockSpec((tm, PROJ_W), lambda i: (i, 0)),
        out_shape=jax.ShapeDtypeStruct((t, PROJ_W), F32),
        compiler_params=pltpu.CompilerParams(
            dimension_semantics=("arbitrary",), vmem_limit_bytes=VMEM_LIMIT),
        name="in_proj",
    )(x2, gain, w)


def _neumann_inverse(l_mat, block):
    n = l_mat.shape[0]
    row = lax.broadcasted_iota(jnp.int32, (n, n), 0)
    col = lax.broadcasted_iota(jnp.int32, (n, n), 1)
    m = -l_mat
    p = jnp.where(row == col, 1.0, 0.0) + m
    for _ in range((block - 1).bit_length() - 1):
        mb = m.astype(BF16)
        m = jnp.dot(mb, mb, preferred_element_type=F32)
        p = p + _mm(p, m)
    return p


def _conv_silu(xbuf, cw_ref, lo):
    win = xbuf[:, lo:lo + LANES]
    w = cw_ref[:, lo:lo + LANES]
    acc = win[SUBLANES:] * w[CONV_K - 1:CONV_K]
    for j in range(CONV_K - 1):
        shifted = pltpu.roll(win, CONV_K - 1 - j, axis=0)[SUBLANES:]
        acc = acc + shifted * w[j:j + 1]
    return _silu(acc)


def _dn_kernel(q_ref, k_ref, v_ref, bd_ref, cwq_ref, cwk_ref, cwv_ref, gp_ref, hn_ref,
               o_ref, xq, xk, xv, state, *, ts, heads):
    hb = pl.program_id(1)
    t = pl.program_id(2)
    nc = ts // CHUNK

    @pl.when(t == 0)
    def _():
        state[...] = jnp.zeros_like(state)
        zero_halo = jnp.zeros((SUBLANES, heads * LANES), F32)
        xq[0:SUBLANES, :] = zero_halo
        xk[0:SUBLANES, :] = zero_halo
        xv[0:SUBLANES, :] = zero_halo

    xq[SUBLANES:, :] = q_ref[...]
    xk[SUBLANES:, :] = k_ref[...]
    xv[SUBLANES:, :] = v_ref[...]

    bd = bd_ref[...]
    beta_all = _sigmoid(bd)
    g_all = -jnp.exp(gp_ref[0:1, :]) * _softplus(bd + gp_ref[1:2, :])
    head_gain = hn_ref[0:1, :]

    lane = lax.broadcasted_iota(jnp.int32, (ts, LANES), 1)
    sub = lax.broadcasted_iota(jnp.int32, (LANES, ts), 0)
    row = lax.broadcasted_iota(jnp.int32, (ts, ts), 0)
    col = lax.broadcasted_iota(jnp.int32, (ts, ts), 1)
    shift = CHUNK.bit_length() - 1
    same = (row >> shift) == (col >> shift)
    causal = same & (row >= col)
    strict = same & (row > col)
    g_cum = _mm_f32(causal.astype(F32), g_all)
    g_cum_t = jnp.transpose(g_cum)
    scale = DN_HEAD_DIM ** -0.5

    for hl in range(heads):
        hg = hb * heads + hl
        lo = hl * LANES
        q = _conv_silu(xq, cwq_ref, lo)
        k = _conv_silu(xk, cwk_ref, lo)
        v = _conv_silu(xv, cwv_ref, lo)
        qn = q * (lax.rsqrt(jnp.sum(q * q, axis=-1, keepdims=True) + EPS) * scale)
        kn = k * lax.rsqrt(jnp.sum(k * k, axis=-1, keepdims=True) + EPS)

        beta = jnp.sum(jnp.where(lane == hg, beta_all, 0.0), axis=-1, keepdims=True)
        g_col = jnp.sum(jnp.where(lane == hg + DECAY_LANE0, g_cum, 0.0), axis=-1, keepdims=True)
        g_row = jnp.sum(jnp.where(sub == hg + DECAY_LANE0, g_cum_t, 0.0), axis=0, keepdims=True)
        decay = jnp.exp(jnp.where(causal, g_col - g_row, -jnp.inf))
        e_g = jnp.exp(g_col)
        g_last = jnp.concatenate(
            [jnp.broadcast_to(g_col[(c + 1) * CHUNK - 1:(c + 1) * CHUNK, :], (CHUNK, 1)) for c in range(nc)],
            axis=0)

        kb = kn * beta
        knb = kn.astype(BF16)
        l_mat = jnp.where(strict, _mm_nt(kb, knb) * decay, 0.0)
        a_qk = _mm_nt(qn, knb) * decay
        t_inv = _neumann_inverse(l_mat, CHUNK)
        rhs = jnp.concatenate([v * beta, kb * e_g], axis=-1)
        sol = _mm(t_inv, rhs)
        a_sol = _mm(a_qk, sol)
        q_eff = qn * e_g - a_sol[:, DN_HEAD_DIM:]
        o_intra = a_sol[:, :DN_HEAD_DIM]
        k_dec = kn * jnp.exp(g_last - g_col)
        gamma = jnp.exp(g_last)

        s = state[hl]
        for c in range(nc):
            rows = slice(c * CHUNK, (c + 1) * CHUNK)
            kt_sol = _mm_tn(k_dec[rows, :], sol[rows, :])
            o = _mm(q_eff[rows, :], s) + o_intra[rows, :]
            s = (s * gamma[c * CHUNK:c * CHUNK + 1, :] - _mm(kt_sol[:, DN_HEAD_DIM:], s)
                 + kt_sol[:, :DN_HEAD_DIM])
            o = o * lax.rsqrt(jnp.mean(o * o, axis=-1, keepdims=True) + EPS) * head_gain
            o_ref[rows, lo:lo + LANES] = o
        state[hl] = s

    xq[0:SUBLANES, :] = xq[ts:ts + SUBLANES, :]
    xk[0:SUBLANES, :] = xk[ts:ts + SUBLANES, :]
    xv[0:SUBLANES, :] = xv[ts:ts + SUBLANES, :]


def _deltanet(proj, conv_w8, gate_p, head_n, bsz, seq, ts, heads):
    nt = seq // ts
    wid = heads * LANES
    nqb = DN_WIDTH // wid
    tok = lambda b, h, t: b * nt + t
    return pl.pallas_call(
        functools.partial(_dn_kernel, ts=ts, heads=heads),
        grid=(bsz, DN_HEADS // heads, nt),
        in_specs=[
            pl.BlockSpec((ts, wid), lambda b, h, t: (tok(b, h, t), h)),
            pl.BlockSpec((ts, wid), lambda b, h, t: (tok(b, h, t), nqb + h)),
            pl.BlockSpec((ts, wid), lambda b, h, t: (tok(b, h, t), 2 * nqb + h)),
            pl.BlockSpec((ts, LANES), lambda b, h, t: (tok(b, h, t), COL_BD)),
            pl.BlockSpec((SUBLANES, wid), lambda b, h, t: (0, h)),
            pl.BlockSpec((SUBLANES, wid), lambda b, h, t: (0, nqb + h)),
            pl.BlockSpec((SUBLANES, wid), lambda b, h, t: (0, 2 * nqb + h)),
            pl.BlockSpec((SUBLANES, LANES), lambda b, h, t: (0, 0)),
            pl.BlockSpec((SUBLANES, LANES), lambda b, h, t: (0, 0)),
        ],
        out_specs=pl.BlockSpec((ts, wid), lambda b, h, t: (tok(b, h, t), h)),
        out_shape=jax.ShapeDtypeStruct((bsz * seq, DN_WIDTH), F32),
        scratch_shapes=[
            pltpu.VMEM((ts + SUBLANES, wid), F32),
            pltpu.VMEM((ts + SUBLANES, wid), F32),
            pltpu.VMEM((ts + SUBLANES, wid), F32),
            pltpu.VMEM((heads, DN_HEAD_DIM, DN_HEAD_DIM), F32),
        ],
        compiler_params=pltpu.CompilerParams(
            dimension_semantics=("arbitrary", "arbitrary", "arbitrary"), vmem_limit_bytes=VMEM_LIMIT),
        name="deltanet",
    )(proj, proj, proj, proj, conv_w8, conv_w8, conv_w8, gate_p, head_n)


def _s5_param_kernel(are, aim, ldt, arer, aimr, ldtr, brt, bit, cimt, lpr, lpi, bbr, bbi, ncim):
    ar = are[0]
    ai = aim[0]
    dt = jnp.exp(ldt[0])
    for k in range(SUBLANES):
        mag = jnp.exp(ar * dt * float(k + 1))
        ang = ai * dt * float(k + 1)
        lpr[0, k] = mag * jnp.cos(ang)
        lpi[0, k] = mag * jnp.sin(ang)
    ar = arer[0]
    ai = aimr[0]
    dt = jnp.exp(ldtr[0])
    mag = jnp.exp(ar * dt)
    lr = mag * jnp.cos(ai * dt)
    li = mag * jnp.sin(ai * dt)
    den = ar * ar + ai * ai
    fr = ((lr - 1.0) * ar + li * ai) / den
    fi = (li * ar - (lr - 1.0) * ai) / den
    br = brt[0]
    bi = bit[0]
    bbr[0] = fr * br - fi * bi
    bbi[0] = fr * bi + fi * br
    ncim[0] = -cimt[0]


def _s5_params(a_re, a_im, log_dt, b_re, b_im, c_im):
    nl = a_re.shape[0]
    g, n, c = SSM_GROUPS, SSM_STATE, SSM_GROUP
    ldt = jnp.broadcast_to(log_dt[..., None], (nl, g, n))
    rep = lambda a: jnp.repeat(a, c, axis=1)
    brt = jnp.swapaxes(b_re, 2, 3).reshape(nl, g * c, n)
    bit = jnp.swapaxes(b_im, 2, 3).reshape(nl, g * c, n)
    cimt = c_im.reshape(nl, g * c, n)
    small = pl.BlockSpec((1, g, n), lambda l: (l, 0, 0))
    big = pl.BlockSpec((1, g * c, n), lambda l: (l, 0, 0))
    powr = pl.BlockSpec((1, SUBLANES, g, n), lambda l: (l, 0, 0, 0))
    return pl.pallas_call(
        _s5_param_kernel,
        grid=(nl,),
        in_specs=[small, small, small, big, big, big, big, big, big],
        out_specs=[powr, powr, big, big, big],
        out_shape=[
            jax.ShapeDtypeStruct((nl, SUBLANES, g, n), F32),
            jax.ShapeDtypeStruct((nl, SUBLANES, g, n), F32),
            jax.ShapeDtypeStruct((nl, g * c, n), F32),
            jax.ShapeDtypeStruct((nl, g * c, n), F32),
            jax.ShapeDtypeStruct((nl, g * c, n), F32),
        ],
        name="s5_params",
    )(a_re, a_im, ldt, rep(a_re), rep(a_im), rep(ldt), brt, bit, cimt)


def _s5_assemble(lpr, lpi, bbr, bbi, c_re, ncim):
    gb = SSM_GROUPS // S5_BLOCKS
    eye = jnp.eye(gb, dtype=F32)

    def bblock(bb):
        b4 = bb.reshape(S5_BLOCKS, gb, SSM_GROUP, SSM_STATE)
        return jnp.einsum("jgcn,gh->jgchn", b4, eye).reshape(S5_BLOCKS, S5_BLK_CH, S5_BLK_ST)

    def cblock(cc):
        c4 = cc.reshape(S5_BLOCKS, gb, SSM_GROUP, SSM_STATE)
        return jnp.einsum("jgcn,gh->jgnhc", c4, eye).reshape(S5_BLOCKS, S5_BLK_ST, S5_BLK_CH)

    bblk = jnp.concatenate([bblock(bbr), bblock(bbi)], axis=-1).astype(BF16)
    cblk = jnp.concatenate([cblock(c_re.reshape(SSM_GROUPS * SSM_GROUP, SSM_STATE)), cblock(ncim)],
                           axis=1).astype(BF16)

    def table(lp):
        lp = lp.reshape(SUBLANES, S5_BLOCKS, S5_BLK_ST)
        r = jnp.arange(SUBLANES)[:, None, None]
        kinds = [jnp.where(r >= s, lp[s - 1][None], 0.0) for s in (1, 2, 4)] + [lp]
        return jnp.transpose(jnp.stack(kinds, axis=0), (2, 0, 1, 3))

    ptab = jnp.concatenate([table(lpr), table(lpi)], axis=-1)
    return bblk, cblk, ptab


def _s5_kernel(u_ref, bblk_ref, cblk_ref, p_ref, d_ref, y_ref, xs, hs, carry, *, ts):
    t = pl.program_id(1)

    @pl.when(t == 0)
    def _():
        carry[...] = jnp.zeros_like(carry)

    u = u_ref[...]
    ub = u.astype(BF16)
    for j in range(S5_BLOCKS):
        ch = slice(j * S5_BLK_CH, (j + 1) * S5_BLK_CH)
        xs[...] = jnp.dot(ub[:, ch], bblk_ref[j], preferred_element_type=F32)
        for s in range(S5_BLK_ST // S5_SLAB):
            re = slice(s * S5_SLAB, (s + 1) * S5_SLAB)
            im = slice(S5_BLK_ST + s * S5_SLAB, S5_BLK_ST + (s + 1) * S5_SLAB)
            steps = [(p_ref[j, i, :, re], p_ref[j, i, :, im], sh) for i, sh in enumerate((1, 2, 4))]
            lam_r = p_ref[j, 3, :, re]
            lam_i = p_ref[j, 3, :, im]

            def body(r, c, re=re, im=im, steps=steps, lam_r=lam_r, lam_i=lam_i):
                cr, ci = c
                r8 = pl.multiple_of(r * SUBLANES, SUBLANES)
                xr = xs[pl.ds(r8, SUBLANES), re]
                xi = xs[pl.ds(r8, SUBLANES), im]
                for pr, pi, sh in steps:
                    sr = pltpu.roll(xr, sh, axis=0)
                    si = pltpu.roll(xi, sh, axis=0)
                    xr, xi = xr + pr * sr - pi * si, xi + pr * si + pi * sr
                hr = xr + lam_r * cr - lam_i * ci
                hi = xi + lam_r * ci + lam_i * cr
                hs[pl.ds(r8, SUBLANES), re] = hr
                hs[pl.ds(r8, SUBLANES), im] = hi
                shape = (SUBLANES, S5_SLAB)
                return (jnp.broadcast_to(hr[SUBLANES - 1:, :], shape),
                        jnp.broadcast_to(hi[SUBLANES - 1:, :], shape))

            cr, ci = lax.fori_loop(0, ts // SUBLANES, body, (carry[j, :, re], carry[j, :, im]), unroll=2)
            carry[j, :, re] = cr
            carry[j, :, im] = ci
        y = jnp.dot(hs[...].astype(BF16), cblk_ref[j], preferred_element_type=F32)
        y_ref[:, ch] = y + d_ref[:, ch] * u[:, ch]


def _s5(proj, bblk, cblk, ptab, d_skip, bsz, seq, ts):
    nt = seq // ts
    const3 = lambda b, t: (0, 0, 0)
    return pl.pallas_call(
        functools.partial(_s5_kernel, ts=ts),
        grid=(bsz, nt),
        in_specs=[
            pl.BlockSpec((ts, D_MODEL), lambda b, t: (b * nt + t, COL_U)),
            pl.BlockSpec((S5_BLOCKS, S5_BLK_CH, 2 * S5_BLK_ST), const3),
            pl.BlockSpec((S5_BLOCKS, 2 * S5_BLK_ST, S5_BLK_CH), const3),
            pl.BlockSpec((S5_BLOCKS, 4, SUBLANES, 2 * S5_BLK_ST), lambda b, t: (0, 0, 0, 0)),
            pl.BlockSpec((1, D_MODEL), lambda b, t: (0, 0)),
        ],
        out_specs=pl.BlockSpec((ts, D_MODEL), lambda b, t: (b * nt + t, 0)),
        out_shape=jax.ShapeDtypeStruct((bsz * seq, D_MODEL), F32),
        scratch_shapes=[
            pltpu.VMEM((ts, 2 * S5_BLK_ST), F32),
            pltpu.VMEM((ts, 2 * S5_BLK_ST), F32),
            pltpu.VMEM((S5_BLOCKS, SUBLANES, 2 * S5_BLK_ST), F32),
        ],
        compiler_params=pltpu.CompilerParams(
            dimension_semantics=("arbitrary", "arbitrary"), vmem_limit_bytes=VMEM_LIMIT),
        name="s5",
    )(proj, bblk, cblk, ptab, d_skip)


def _merge_kernel(ya_ref, za_ref, ys_ref, zb_ref, ra_ref, rb_ref, x_ref, wglu_ref, bglu_ref,
                  wout_ref, np_ref, o_ref):
    y_a = ya_ref[...] * _silu(za_ref[...])
    y = _gelu_tanh(ys_ref[...])
    y = y * _sigmoid(jnp.dot(y.astype(BF16), wglu_ref[...], preferred_element_type=F32) + bglu_ref[...])
    y_b = y * _silu(zb_ref[...])
    merged = _sigmoid(ra_ref[...]) * y_a + _sigmoid(rb_ref[...]) * y_b
    out = jnp.dot(merged.astype(BF16), wout_ref[...], preferred_element_type=F32)
    out = out * lax.rsqrt(jnp.mean(out * out, axis=-1, keepdims=True) + EPS) * np_ref[...]
    o_ref[...] = x_ref[...] + out


def _merge(y_a, y_s, proj, x2, w_glu, b_glu, w_out, norm_post, tm):
    t = x2.shape[0]
    col = lambda c: pl.BlockSpec((tm, D_MODEL), lambda i: (i, c))
    vec = pl.BlockSpec((1, D_MODEL), lambda i: (0, 0))
    mat = pl.BlockSpec((D_MODEL, D_MODEL), lambda i: (0, 0))
    return pl.pallas_call(
        _merge_kernel,
        grid=(t // tm,),
        in_specs=[col(0), col(COL_ZA), col(0), col(COL_ZB), col(COL_RA), col(COL_RB), col(0),
                  mat, vec, mat, vec],
        out_specs=col(0),
        out_shape=jax.ShapeDtypeStruct((t, D_MODEL), F32),
        compiler_params=pltpu.CompilerParams(
            dimension_semantics=("arbitrary",), vmem_limit_bytes=VMEM_LIMIT),
        name="merge",
    )(y_a, proj, y_s, proj, proj, proj, x2, w_glu, b_glu, w_out, norm_post)


def _prep_w_in(w_in):
    o = 3 * DN_WIDTH
    qkv, za = w_in[..., :o], w_in[..., o:o + DN_WIDTH]
    o += DN_WIDTH
    bd = w_in[..., o:o + 2 * DN_HEADS]
    o += 2 * DN_HEADS
    u, zb, ra, rb = (w_in[..., o + i * D_MODEL:o + (i + 1) * D_MODEL] for i in range(4))
    pad = jnp.zeros(w_in.shape[:-1] + (LANES - 2 * DN_HEADS,), w_in.dtype)
    return jnp.concatenate([qkv, za, u, zb, ra, rb, bd, pad], axis=-1).astype(BF16)


def _pad_rows(a, rows=SUBLANES):
    return jnp.concatenate([a, jnp.zeros((rows - a.shape[0],) + a.shape[1:], a.dtype)], axis=0)


def _trunk(x, norm_pre, w_in, conv_w, a_log, dt_bias, head_norm, ssm_a_re, ssm_a_im, ssm_log_dt,
           ssm_b_re, ssm_b_im, ssm_c_re, ssm_c_im, ssm_d, w_glu, b_glu, w_out, norm_post,
           *, tm=256, ts_dn=256, dn_heads=8, ts_s5=256):
    bsz, seq, _ = x.shape
    depth = w_in.shape[0]
    x2 = x.reshape(bsz * seq, D_MODEL)
    w_in_r = _prep_w_in(w_in)
    w_glu_b = w_glu.astype(BF16)
    w_out_b = w_out.astype(BF16)
    lpr, lpi, bbr, bbi, ncim = _s5_params(ssm_a_re, ssm_a_im, ssm_log_dt, ssm_b_re, ssm_b_im, ssm_c_im)
    lane_pad = jnp.zeros((LANES - 2 * DN_HEADS,), F32)
    for i in range(depth):
        proj = _in_proj(x2, norm_pre[i][None], w_in_r[i], tm)
        gate_p = _pad_rows(jnp.stack([
            jnp.concatenate([jnp.zeros((DN_HEADS,), F32), a_log[i], lane_pad]),
            jnp.concatenate([jnp.zeros((DN_HEADS,), F32), dt_bias[i], lane_pad])]))
        y_a = _deltanet(proj, _pad_rows(conv_w[i]), gate_p, _pad_rows(head_norm[i][None]),
                        bsz, seq, ts_dn, dn_heads)
        bblk, cblk, ptab = _s5_assemble(lpr[i], lpi[i], bbr[i], bbi[i], ssm_c_re[i], ncim[i])
        y_s = _s5(proj, bblk, cblk, ptab, ssm_d[i][None], bsz, seq, ts_s5)
        x2 = _merge(y_a, y_s, proj, x2, w_glu_b[i], b_glu[i][None], w_out_b[i], norm_post[i][None], tm)
    return x2.reshape(bsz, seq, D_MODEL)


def kernel(x, norm_pre, w_in, conv_w, a_log, dt_bias, head_norm, ssm_a_re, ssm_a_im, ssm_log_dt,
           ssm_b_re, ssm_b_im, ssm_c_re, ssm_c_im, ssm_d, w_glu, b_glu, w_out, norm_post):
    return _trunk(x, norm_pre, w_in, conv_w, a_log, dt_bias, head_norm, ssm_a_re, ssm_a_im, ssm_log_dt,
                  ssm_b_re, ssm_b_im, ssm_c_re, ssm_c_im, ssm_d, w_glu, b_glu, w_out, norm_post)
```

```python
import functools

import jax
import jax.numpy as jnp
from jax import lax
from jax.experimental import pallas as pl
from jax.experimental.pallas import tpu as pltpu

D_MODEL = 1024
DN_HEADS = 8
DN_HEAD_DIM = 128
DN_WIDTH = DN_HEADS * DN_HEAD_DIM
CONV_K = 4
CHUNK = 64
SSM_GROUP = 16
SSM_GROUPS = 64
SSM_STATE = 64
EPS = 1e-6

LANES = 128
SUBLANES = 8
VMEM_LIMIT = 56 * 1024 * 1024

PROJ_W = 8 * D_MODEL + LANES
COL_ZA, COL_U, COL_ZB, COL_RA, COL_RB = 3, 4, 5, 6, 7
COL_BD = 8 * D_MODEL // LANES
DECAY_LANE0 = DN_HEADS

S5_L = 16
S5_LC = S5_L * SSM_GROUP
S5_GB = LANES // SSM_GROUP
S5_NPOW = 8

F32 = jnp.float32
BF16 = jnp.bfloat16
HIGHEST = lax.Precision.HIGHEST


def _sigmoid(x):
    return 1.0 / (1.0 + jnp.exp(-x))


def _silu(x):
    return x * _sigmoid(x)


def _gelu_tanh(x):
    c = 0.7978845608028654
    return 0.5 * x * (1.0 + jnp.tanh(c * (x + 0.044715 * (x * x * x))))


def _softplus(x):
    return jnp.maximum(x, 0.0) + jnp.log(1.0 + jnp.exp(-jnp.abs(x)))


def _mm(a, b):
    return jnp.dot(a.astype(BF16), b.astype(BF16), preferred_element_type=F32)


def _mm_nt(a, b):
    return lax.dot_general(a.astype(BF16), b.astype(BF16), (((1,), (1,)), ((), ())),
                           preferred_element_type=F32)


def _mm_tn(a, b):
    return lax.dot_general(a.astype(BF16), b.astype(BF16), (((0,), (0,)), ((), ())),
                           preferred_element_type=F32)


def _mm_f32(a, b):
    return jnp.dot(a, b, precision=HIGHEST, preferred_element_type=F32)


def _mm_nt_f32(a, b):
    return lax.dot_general(a, b, (((1,), (1,)), ((), ())), precision=HIGHEST,
                           preferred_element_type=F32)


def _in_proj_kernel(x_ref, g_ref, w_ref, o_ref):
    x = x_ref[...]
    h = x * lax.rsqrt(jnp.mean(x * x, axis=-1, keepdims=True) + EPS) * g_ref[...]
    hb = h.astype(BF16)
    for a in range(0, PROJ_W, D_MODEL):
        b = min(a + D_MODEL, PROJ_W)
        o_ref[:, a:b] = jnp.dot(hb, w_ref[:, a:b], preferred_element_type=F32)


def _in_proj(x2, gain, w, tm):
    t = x2.shape[0]
    return pl.pallas_call(
        _in_proj_kernel,
        grid=(t // tm,),
        in_specs=[
            pl.BlockSpec((tm, D_MODEL), lambda i: (i, 0)),
            pl.BlockSpec((1, D_MODEL), lambda i: (0, 0)),
            pl.BlockSpec((D_MODEL, PROJ_W), lambda i: (0, 0), pipeline_mode=pl.Buffered(1)),
        ],
        out_specs=pl.BlockSpec((tm, PROJ_W), lambda i: (i, 0)),
        out_shape=jax.ShapeDtypeStruct((t, PROJ_W), F32),
        compiler_params=pltpu.CompilerParams(
            dimension_semantics=("arbitrary",), vmem_limit_bytes=VMEM_LIMIT),
        name="in_proj",
    )(x2, gain, w)


def _neumann_inverse(l_mat, block):
    n = l_mat.shape[0]
    row = lax.broadcasted_iota(jnp.int32, (n, n), 0)
    col = lax.broadcasted_iota(jnp.int32, (n, n), 1)
    m = -l_mat
    p = jnp.where(row == col, 1.0, 0.0) + m
    for _ in range((block - 1).bit_length() - 1):
        mb = m.astype(BF16)
        m = jnp.dot(mb, mb, preferred_element_type=F32)
        p = p + _mm(p, m)
    return p


def _conv_silu(xbuf, cw_ref, lo):
    win = xbuf[:, lo:lo + LANES]
    w = cw_ref[:, lo:lo + LANES]
    acc = win[SUBLANES:] * w[CONV_K - 1:CONV_K]
    for j in range(CONV_K - 1):
        shifted = pltpu.roll(win, CONV_K - 1 - j, axis=0)[SUBLANES:]
        acc = acc + shifted * w[j:j + 1]
    return _silu(acc)


def _dn_kernel(q_ref, k_ref, v_ref, bd_ref, cwq_ref, cwk_ref, cwv_ref, gp_ref, hn_ref,
               o_ref, xq, xk, xv, state, *, ts, heads):
    hb = pl.program_id(1)
    t = pl.program_id(2)
    nc = ts // CHUNK

    @pl.when(t == 0)
    def _():
        state[...] = jnp.zeros_like(state)
        zero_halo = jnp.zeros((SUBLANES, heads * LANES), F32)
        xq[0:SUBLANES, :] = zero_halo
        xk[0:SUBLANES, :] = zero_halo
        xv[0:SUBLANES, :] = zero_halo

    xq[SUBLANES:, :] = q_ref[...]
    xk[SUBLANES:, :] = k_ref[...]
    xv[SUBLANES:, :] = v_ref[...]

    bd = bd_ref[...]
    beta_all = _sigmoid(bd)
    g_all = -jnp.exp(gp_ref[0:1, :]) * _softplus(bd + gp_ref[1:2, :])
    head_gain = hn_ref[0:1, :]

    lane = lax.broadcasted_iota(jnp.int32, (ts, LANES), 1)
    sub = lax.broadcasted_iota(jnp.int32, (LANES, ts), 0)
    row = lax.broadcasted_iota(jnp.int32, (ts, ts), 0)
    col = lax.broadcasted_iota(jnp.int32, (ts, ts), 1)
    shift = CHUNK.bit_length() - 1
    same = (row >> shift) == (col >> shift)
    causal = same & (row >= col)
    strict = same & (row > col)
    g_cum = _mm_f32(causal.astype(F32), g_all)
    g_cum_t = jnp.transpose(g_cum)
    scale = DN_HEAD_DIM ** -0.5

    for hl in range(heads):
        hg = hb * heads + hl
        lo = hl * LANES
        q = _conv_silu(xq, cwq_ref, lo)
        k = _conv_silu(xk, cwk_ref, lo)
        v = _conv_silu(xv, cwv_ref, lo)
        qn = q * (lax.rsqrt(jnp.sum(q * q, axis=-1, keepdims=True) + EPS) * scale)
        kn = k * lax.rsqrt(jnp.sum(k * k, axis=-1, keepdims=True) + EPS)

        beta = jnp.sum(jnp.where(lane == hg, beta_all, 0.0), axis=-1, keepdims=True)
        g_col = jnp.sum(jnp.where(lane == hg + DECAY_LANE0, g_cum, 0.0), axis=-1, keepdims=True)
        g_row = jnp.sum(jnp.where(sub == hg + DECAY_LANE0, g_cum_t, 0.0), axis=0, keepdims=True)
        decay = jnp.exp(jnp.where(causal, g_col - g_row, -jnp.inf))
        e_g = jnp.exp(g_col)
        g_last = jnp.concatenate(
            [jnp.broadcast_to(g_col[(c + 1) * CHUNK - 1:(c + 1) * CHUNK, :], (CHUNK, 1)) for c in range(nc)],
            axis=0)

        kb = kn * beta
        knb = kn.astype(BF16)
        l_mat = jnp.where(strict, _mm_nt(kb, knb) * decay, 0.0)
        a_qk = _mm_nt(qn, knb) * decay
        t_inv = _neumann_inverse(l_mat, CHUNK)
        rhs = jnp.concatenate([v * beta, kb * e_g], axis=-1)
        sol = _mm(t_inv, rhs)
        a_sol = _mm(a_qk, sol)
        q_eff = qn * e_g - a_sol[:, DN_HEAD_DIM:]
        o_intra = a_sol[:, :DN_HEAD_DIM]
        k_dec = kn * jnp.exp(g_last - g_col)
        gamma = jnp.exp(g_last)

        s = state[hl]
        for c in range(nc):
            rows = slice(c * CHUNK, (c + 1) * CHUNK)
            kt_sol = _mm_tn(k_dec[rows, :], sol[rows, :])
            o = _mm(q_eff[rows, :], s) + o_intra[rows, :]
            s = (s * gamma[c * CHUNK:c * CHUNK + 1, :] - _mm(kt_sol[:, DN_HEAD_DIM:], s)
                 + kt_sol[:, :DN_HEAD_DIM])
            o = o * lax.rsqrt(jnp.mean(o * o, axis=-1, keepdims=True) + EPS) * head_gain
            o_ref[rows, lo:lo + LANES] = o
        state[hl] = s

    xq[0:SUBLANES, :] = xq[ts:ts + SUBLANES, :]
    xk[0:SUBLANES, :] = xk[ts:ts + SUBLANES, :]
    xv[0:SUBLANES, :] = xv[ts:ts + SUBLANES, :]


def _deltanet(proj, conv_w8, gate_p, head_n, bsz, seq, ts, heads):
    nt = seq // ts
    wid = heads * LANES
    nqb = DN_WIDTH // wid
    tok = lambda b, h, t: b * nt + t
    return pl.pallas_call(
        functools.partial(_dn_kernel, ts=ts, heads=heads),
        grid=(bsz, DN_HEADS // heads, nt),
        in_specs=[
            pl.BlockSpec((ts, wid), lambda b, h, t: (tok(b, h, t), h)),
            pl.BlockSpec((ts, wid), lambda b, h, t: (tok(b, h, t), nqb + h)),
            pl.BlockSpec((ts, wid), lambda b, h, t: (tok(b, h, t), 2 * nqb + h)),
            pl.BlockSpec((ts, LANES), lambda b, h, t: (tok(b, h, t), COL_BD)),
            pl.BlockSpec((SUBLANES, wid), lambda b, h, t: (0, h)),
            pl.BlockSpec((SUBLANES, wid), lambda b, h, t: (0, nqb + h)),
            pl.BlockSpec((SUBLANES, wid), lambda b, h, t: (0, 2 * nqb + h)),
            pl.BlockSpec((SUBLANES, LANES), lambda b, h, t: (0, 0)),
            pl.BlockSpec((SUBLANES, LANES), lambda b, h, t: (0, 0)),
        ],
        out_specs=pl.BlockSpec((ts, wid), lambda b, h, t: (tok(b, h, t), h)),
        out_shape=jax.ShapeDtypeStruct((bsz * seq, DN_WIDTH), F32),
        scratch_shapes=[
            pltpu.VMEM((ts + SUBLANES, wid), F32),
            pltpu.VMEM((ts + SUBLANES, wid), F32),
            pltpu.VMEM((ts + SUBLANES, wid), F32),
            pltpu.VMEM((heads, DN_HEAD_DIM, DN_HEAD_DIM), F32),
        ],
        compiler_params=pltpu.CompilerParams(
            dimension_semantics=("arbitrary", "arbitrary", "arbitrary"), vmem_limit_bytes=VMEM_LIMIT),
        name="deltanet",
    )(proj, proj, proj, proj, conv_w8, conv_w8, conv_w8, gate_p, head_n)


def _s5_param_kernel(are, aim, ldt, arer, aimr, ldtr, brt, bit, pwr, pwi, sqr, sqi, nsqi, bbr, bbi):
    ar = are[0]
    ai = aim[0]
    dt = jnp.exp(ldt[0])
    for k in range(S5_L + 1):
        mag = jnp.exp(ar * dt * float(k))
        ang = ai * dt * float(k)
        pwr[0, k] = mag * jnp.cos(ang)
        pwi[0, k] = mag * jnp.sin(ang)
    pr = pwr[0, S5_L]
    pi = pwi[0, S5_L]
    for j in range(S5_NPOW):
        sqr[0, j] = pr
        sqi[0, j] = pi
        nsqi[0, j] = -pi
        pr, pi = pr * pr - pi * pi, 2.0 * (pr * pi)
    ar = arer[0]
    ai = aimr[0]
    dt = jnp.exp(ldtr[0])
    mag = jnp.exp(ar * dt)
    lr = mag * jnp.cos(ai * dt)
    li = mag * jnp.sin(ai * dt)
    den = ar * ar + ai * ai
    fr = ((lr - 1.0) * ar + li * ai) / den
    fi = (li * ar - (lr - 1.0) * ai) / den
    br = brt[0]
    bi = bit[0]
    bbr[0] = fr * br - fi * bi
    bbi[0] = fr * bi + fi * br


def _s5_params(a_re, a_im, log_dt, b_re, b_im):
    nl = a_re.shape[0]
    g, n, c = SSM_GROUPS, SSM_STATE, SSM_GROUP
    ldt = jnp.broadcast_to(log_dt[..., None], (nl, g, n))
    rep = lambda a: jnp.repeat(a, c, axis=1)
    brt = jnp.swapaxes(b_re, 2, 3).reshape(nl, g * c, n)
    bit = jnp.swapaxes(b_im, 2, 3).reshape(nl, g * c, n)
    small = pl.BlockSpec((1, g, n), lambda l: (l, 0, 0))
    big = pl.BlockSpec((1, g * c, n), lambda l: (l, 0, 0))
    powr = pl.BlockSpec((1, S5_L + 1, g, n), lambda l: (l, 0, 0, 0))
    sqs = pl.BlockSpec((1, S5_NPOW, g, n), lambda l: (l, 0, 0, 0))
    pw_shape = jax.ShapeDtypeStruct((nl, S5_L + 1, g, n), F32)
    sq_shape = jax.ShapeDtypeStruct((nl, S5_NPOW, g, n), F32)
    bb_shape = jax.ShapeDtypeStruct((nl, g * c, n), F32)
    return pl.pallas_call(
        _s5_param_kernel,
        grid=(nl,),
        in_specs=[small, small, small, big, big, big, big, big],
        out_specs=[powr, powr, sqs, sqs, sqs, big, big],
        out_shape=[pw_shape, pw_shape, sq_shape, sq_shape, sq_shape, bb_shape, bb_shape],
        name="s5_params",
    )(a_re, a_im, ldt, rep(a_re), rep(a_im), rep(ldt), brt, bit)


def _s5_toeplitz_kernel(lpr, lpi, lvr, lvi, cqr, cqi, btr, bti, lwr, lwi, bwr, bwi,
                        kall, vr, nvi, wr, wi):
    for gl in range(S5_GB):
        cr = cqr[0, gl]
        ci = cqi[0, gl]
        pr = cr * lpr[0, gl] - ci * lpi[0, gl]
        pi = cr * lpi[0, gl] + ci * lpr[0, gl]
        kall[0, gl] = _mm_f32(btr[0, gl], pr) - _mm_f32(bti[0, gl], pi)
        qr = cr * lvr[0, gl] - ci * lvi[0, gl]
        qi = cr * lvi[0, gl] + ci * lvr[0, gl]
        vr[0, gl] = qr
        nvi[0, gl] = -qi
        wr[0, gl] = lwr[0, gl] * bwr[0, gl] - lwi[0, gl] * bwi[0, gl]
        wi[0, gl] = lwr[0, gl] * bwi[0, gl] + lwi[0, gl] * bwr[0, gl]


def _s5_weights(pwr, pwi, sqr, sqi, nsqi, bbr, bbi, c_re, c_im):
    nl = pwr.shape[0]
    g, n, c, ln = SSM_GROUPS, SSM_STATE, SSM_GROUP, S5_L
    to_lanes = lambda p: jnp.repeat(jnp.transpose(p, (0, 2, 3, 1)), c, axis=-1)
    lpr, lpi = to_lanes(pwr[:, :ln]), to_lanes(pwi[:, :ln])
    lvr, lvi = to_lanes(pwr[:, 1:]), to_lanes(pwi[:, 1:])
    tile_c = lambda cc: jnp.tile(jnp.swapaxes(cc, 2, 3), (1, 1, 1, ln))
    cqr, cqi = tile_c(c_re), tile_c(c_im)
    btr, bti = bbr.reshape(nl, g, c, n), bbi.reshape(nl, g, c, n)
    to_rows = lambda p: jnp.repeat(jnp.transpose(p[:, :ln][:, ::-1], (0, 2, 1, 3)), c, axis=2)
    lwr, lwi = to_rows(pwr), to_rows(pwi)
    bwr, bwi = jnp.tile(btr, (1, 1, ln, 1)), jnp.tile(bti, (1, 1, ln, 1))

    spec = lambda a, b: pl.BlockSpec((1, S5_GB, a, b), lambda l, q: (l, q, 0, 0))
    shp = lambda a, b: jax.ShapeDtypeStruct((nl, g, a, b), F32)
    kall, vr, nvi, wr, wi = pl.pallas_call(
        _s5_toeplitz_kernel,
        grid=(nl, g // S5_GB),
        in_specs=[spec(n, S5_LC)] * 6 + [spec(c, n)] * 2 + [spec(S5_LC, n)] * 4,
        out_specs=[spec(c, S5_LC), spec(n, S5_LC), spec(n, S5_LC), spec(S5_LC, n), spec(S5_LC, n)],
        out_shape=[shp(c, S5_LC), shp(n, S5_LC), shp(n, S5_LC), shp(S5_LC, n), shp(S5_LC, n)],
        name="s5_toeplitz",
    )(lpr, lpi, lvr, lvi, cqr, cqi, btr, bti, lwr, lwi, bwr, bwi)

    rows = [jnp.pad(kall[..., :S5_LC - c * s], ((0, 0), (0, 0), (0, 0), (c * s, 0))) for s in range(ln)]
    toep = jnp.stack(rows, axis=2).reshape(nl, g, S5_LC, S5_LC)
    tw = jnp.concatenate([toep, wr, wi], axis=-1).astype(BF16)
    v = jnp.concatenate([vr, nvi], axis=2).astype(BF16)
    pair = lambda a, b: jnp.concatenate([jnp.transpose(a, (0, 2, 1, 3)), jnp.transpose(b, (0, 2, 1, 3))], axis=-1)
    sc = jnp.concatenate([pair(sqr, sqr), pair(nsqi, sqi)], axis=-1)
    return tw, v, sc


def _granule_transpose(xs):
    xs = list(xs)
    lane = lax.broadcasted_iota(jnp.int32, xs[0].shape, 1)
    gshift = SSM_GROUP.bit_length() - 1
    for m in range(S5_GB.bit_length() - 1):
        d = SSM_GROUP << m
        upper = ((lane >> (gshift + m)) & 1) == 1
        nxt = list(xs)
        for ia in range(S5_GB):
            if (ia >> m) & 1:
                continue
            ib = ia | (1 << m)
            a, b = xs[ia], xs[ib]
            nxt[ia] = jnp.where(upper, pltpu.roll(b, d, axis=1), a)
            nxt[ib] = jnp.where(upper, b, pltpu.roll(a, LANES - d, axis=1))
        xs = nxt
    return xs


def _s5_kernel(u_ref, tw_ref, v_ref, sc_ref, d_ref, y_ref, carry, *, ts):
    t = pl.program_id(1)
    i = pl.program_id(2)
    r = ts // S5_L
    half = SSM_STATE

    @pl.when(t == 0)
    def _():
        carry[pl.ds(i * S5_GB, S5_GB)] = jnp.zeros((S5_GB, SUBLANES, LANES), F32)

    a_rows = [u_ref[pl.ds(s, r, stride=S5_L), :] for s in range(S5_L)]
    folded = [_granule_transpose(a_rows[h * S5_GB:(h + 1) * S5_GB]) for h in range(S5_L // S5_GB)]
    row = lax.broadcasted_iota(jnp.int32, (r, LANES), 0)
    row8 = lax.broadcasted_iota(jnp.int32, (SUBLANES, LANES), 0)
    y_groups = []
    for gl in range(S5_GB):
        g = i * S5_GB + gl
        u_g = jnp.concatenate([f[gl] for f in folded], axis=1).astype(BF16)
        tx = jnp.dot(u_g, tw_ref[gl], preferred_element_type=F32)
        x = tx[:, S5_LC:]
        sc = sc_ref[gl]
        c8 = carry[g]
        inj = sc[0:1, :LANES] * c8 + sc[0:1, LANES:] * pltpu.roll(c8, half, axis=1)
        h = jnp.concatenate([x[:SUBLANES] + jnp.where(row8 == 0, inj, 0.0), x[SUBLANES:]], axis=0)
        for j in range(r.bit_length() - 1):
            sh = 1 << j
            hs = jnp.where(row >= sh, pltpu.roll(h, sh, axis=0), 0.0)
            h = h + sc[j:j + 1, :LANES] * hs + sc[j:j + 1, LANES:] * pltpu.roll(hs, half, axis=1)
        h_prev = jnp.where(row >= 1, pltpu.roll(h, 1, axis=0), jnp.broadcast_to(c8[0:1], (r, LANES)))
        y_groups.append(tx[:, :S5_LC] + jnp.dot(h_prev.astype(BF16), v_ref[gl], preferred_element_type=F32))
        carry[g] = jnp.broadcast_to(h[r - 1:r], (SUBLANES, LANES))
    d_skip = d_ref[...]
    for h in range(S5_L // S5_GB):
        unfolded = _granule_transpose([y[:, h * LANES:(h + 1) * LANES] for y in y_groups])
        for q in range(S5_GB):
            s = h * S5_GB + q
            y_ref[pl.ds(s, r, stride=S5_L), :] = unfolded[q] + d_skip * a_rows[s]


def _s5(proj, tw, v, sc, d_skip, bsz, seq, ts):
    nt = seq // ts
    ncb = D_MODEL // LANES
    grp = lambda a, b: pl.BlockSpec((S5_GB, a, b), lambda b_, t, i: (i, 0, 0))
    return pl.pallas_call(
        functools.partial(_s5_kernel, ts=ts),
        grid=(bsz, nt, ncb),
        in_specs=[
            pl.BlockSpec((ts, LANES), lambda b, t, i: (b * nt + t, COL_U * ncb + i)),
            grp(S5_LC, S5_LC + 2 * SSM_STATE),
            grp(2 * SSM_STATE, S5_LC),
            grp(S5_NPOW, 2 * LANES),
            pl.BlockSpec((1, LANES), lambda b, t, i: (0, i)),
        ],
        out_specs=pl.BlockSpec((ts, LANES), lambda b, t, i: (b * nt + t, i)),
        out_shape=jax.ShapeDtypeStruct((bsz * seq, D_MODEL), F32),
        scratch_shapes=[pltpu.VMEM((SSM_GROUPS, SUBLANES, LANES), F32)],
        compiler_params=pltpu.CompilerParams(
            dimension_semantics=("arbitrary", "arbitrary", "arbitrary"), vmem_limit_bytes=VMEM_LIMIT),
        name="s5",
    )(proj, tw, v, sc, d_skip)


def _merge_kernel(ya_ref, za_ref, ys_ref, zb_ref, ra_ref, rb_ref, x_ref, wglu_ref, bglu_ref,
                  wout_ref, np_ref, o_ref):
    y_a = ya_ref[...] * _silu(za_ref[...])
    y = _gelu_tanh(ys_ref[...])
    y = y * _sigmoid(jnp.dot(y.astype(BF16), wglu_ref[...], preferred_element_type=F32) + bglu_ref[...])
    y_b = y * _silu(zb_ref[...])
    merged = _sigmoid(ra_ref[...]) * y_a + _sigmoid(rb_ref[...]) * y_b
    out = jnp.dot(merged.astype(BF16), wout_ref[...], preferred_element_type=F32)
    out = out * lax.rsqrt(jnp.mean(out * out, axis=-1, keepdims=True) + EPS) * np_ref[...]
    o_ref[...] = x_ref[...] + out


def _merge(y_a, y_s, proj, x2, w_glu, b_glu, w_out, norm_post, tm):
    t = x2.shape[0]
    col = lambda c: pl.BlockSpec((tm, D_MODEL), lambda i: (i, c))
    vec = pl.BlockSpec((1, D_MODEL), lambda i: (0, 0))
    mat = pl.BlockSpec((D_MODEL, D_MODEL), lambda i: (0, 0))
    return pl.pallas_call(
        _merge_kernel,
        grid=(t // tm,),
        in_specs=[col(0), col(COL_ZA), col(0), col(COL_ZB), col(COL_RA), col(COL_RB), col(0),
                  mat, vec, mat, vec],
        out_specs=col(0),
        out_shape=jax.ShapeDtypeStruct((t, D_MODEL), F32),
        compiler_params=pltpu.CompilerParams(
            dimension_semantics=("arbitrary",), vmem_limit_bytes=VMEM_LIMIT),
        name="merge",
    )(y_a, proj, y_s, proj, proj, proj, x2, w_glu, b_glu, w_out, norm_post)


def _prep_w_in(w_in):
    o = 3 * DN_WIDTH
    qkv, za = w_in[..., :o], w_in[..., o:o + DN_WIDTH]
    o += DN_WIDTH
    bd = w_in[..., o:o + 2 * DN_HEADS]
    o += 2 * DN_HEADS
    u, zb, ra, rb = (w_in[..., o + i * D_MODEL:o + (i + 1) * D_MODEL] for i in range(4))
    pad = jnp.zeros(w_in.shape[:-1] + (LANES - 2 * DN_HEADS,), w_in.dtype)
    return jnp.concatenate([qkv, za, u, zb, ra, rb, bd, pad], axis=-1).astype(BF16)


def _pad_rows(a, rows=SUBLANES):
    return jnp.concatenate([a, jnp.zeros((rows - a.shape[0],) + a.shape[1:], a.dtype)], axis=0)


def _trunk(x, norm_pre, w_in, conv_w, a_log, dt_bias, head_norm, ssm_a_re, ssm_a_im, ssm_log_dt,
           ssm_b_re, ssm_b_im, ssm_c_re, ssm_c_im, ssm_d, w_glu, b_glu, w_out, norm_post,
           *, tm=256, ts_dn=256, dn_heads=8, ts_s5=2048):
    bsz, seq, _ = x.shape
    depth = w_in.shape[0]
    ts_s5 = min(ts_s5, seq)
    x2 = x.reshape(bsz * seq, D_MODEL)
    w_in_r = _prep_w_in(w_in)
    w_glu_b = w_glu.astype(BF16)
    w_out_b = w_out.astype(BF16)
    s5_tw, s5_v, s5_sc = _s5_weights(*_s5_params(ssm_a_re, ssm_a_im, ssm_log_dt, ssm_b_re, ssm_b_im),
                                     ssm_c_re, ssm_c_im)
    lane_pad = jnp.zeros((LANES - 2 * DN_HEADS,), F32)
    for i in range(depth):
        proj = _in_proj(x2, norm_pre[i][None], w_in_r[i], tm)
        gate_p = _pad_rows(jnp.stack([
            jnp.concatenate([jnp.zeros((DN_HEADS,), F32), a_log[i], lane_pad]),
            jnp.concatenate([jnp.zeros((DN_HEADS,), F32), dt_bias[i], lane_pad])]))
        y_a = _deltanet(proj, _pad_rows(conv_w[i]), gate_p, _pad_rows(head_norm[i][None]),
                        bsz, seq, ts_dn, dn_heads)
        y_s = _s5(proj, s5_tw[i], s5_v[i], s5_sc[i], ssm_d[i][None], bsz, seq, ts_s5)
        x2 = _merge(y_a, y_s, proj, x2, w_glu_b[i], b_glu[i][None], w_out_b[i], norm_post[i][None], tm)
    return x2.reshape(bsz, seq, D_MODEL)


def kernel(x, norm_pre, w_in, conv_w, a_log, dt_bias, head_norm, ssm_a_re, ssm_a_im, ssm_log_dt,
           ssm_b_re, ssm_b_im, ssm_c_re, ssm_c_im, ssm_d, w_glu, b_glu, w_out, norm_post):
    return _trunk(x, norm_pre, w_in, conv_w, a_log, dt_bias, head_norm, ssm_a_re, ssm_a_im, ssm_log_dt,
                  ssm_b_re, ssm_b_im, ssm_c_re, ssm_c_im, ssm_d, w_glu, b_glu, w_out, norm_post)
```

```python
import functools

import jax
import jax.numpy as jnp
from jax import lax
from jax.experimental import pallas as pl
from jax.experimental.pallas import tpu as pltpu

D_MODEL = 1024
DN_HEADS = 8
DN_HEAD_DIM = 128
DN_WIDTH = DN_HEADS * DN_HEAD_DIM
CONV_K = 4
CHUNK = 64
SSM_GROUP = 16
SSM_GROUPS = 64
SSM_STATE = 64
EPS = 1e-6

LANES = 128
SUBLANES = 8
VMEM_LIMIT = 56 * 1024 * 1024

PROJ_W = 8 * D_MODEL + LANES
COL_ZA, COL_U, COL_ZB, COL_RA, COL_RB = 3, 4, 5, 6, 7
COL_BD = 8 * D_MODEL // LANES
DECAY_LANE0 = DN_HEADS

S5_L = 16
S5_LC = S5_L * SSM_GROUP
S5_GB = LANES // SSM_GROUP
S5_NPOW = 8

F32 = jnp.float32
BF16 = jnp.bfloat16
HIGHEST = lax.Precision.HIGHEST


def _sigmoid(x):
    return 1.0 / (1.0 + jnp.exp(-x))


def _silu(x):
    return x * _sigmoid(x)


def _gelu_tanh(x):
    c = 0.7978845608028654
    return 0.5 * x * (1.0 + jnp.tanh(c * (x + 0.044715 * (x * x * x))))


def _softplus(x):
    return jnp.maximum(x, 0.0) + jnp.log(1.0 + jnp.exp(-jnp.abs(x)))


def _mm(a, b):
    return jnp.dot(a.astype(BF16), b.astype(BF16), preferred_element_type=F32)


def _mm_nt(a, b):
    return lax.dot_general(a.astype(BF16), b.astype(BF16), (((1,), (1,)), ((), ())),
                           preferred_element_type=F32)


def _mm_tn(a, b):
    return lax.dot_general(a.astype(BF16), b.astype(BF16), (((0,), (0,)), ((), ())),
                           preferred_element_type=F32)


def _mm_f32(a, b):
    return jnp.dot(a, b, precision=HIGHEST, preferred_element_type=F32)


def _mm_nt_f32(a, b):
    return lax.dot_general(a, b, (((1,), (1,)), ((), ())), precision=HIGHEST,
                           preferred_element_type=F32)


def _in_proj_kernel(x_ref, g_ref, w_ref, cw_ref, gp_ref, o_ref, halo, *, tm, tiles_per_seq):
    i = pl.program_id(0)

    @pl.when(i % tiles_per_seq == 0)
    def _():
        halo[...] = jnp.zeros_like(halo)

    x = x_ref[...]
    h = x * lax.rsqrt(jnp.mean(x * x, axis=-1, keepdims=True) + EPS) * g_ref[...]
    hb = h.astype(BF16)
    scale = DN_HEAD_DIM ** -0.5
    wins = []
    for kind in range(3):
        a = kind * DN_WIDTH
        raw = jnp.dot(hb, w_ref[:, a:a + DN_WIDTH], preferred_element_type=F32)
        wins.append(jnp.concatenate([halo[kind], raw], axis=0))
        halo[kind] = raw[tm - SUBLANES:, :]

    def conv_heads(kind, heads):
        a = kind * DN_WIDTH
        for hl in heads:
            lo = hl * LANES
            ws = wins[kind][:, lo:lo + LANES]
            w = cw_ref[:, a + lo:a + lo + LANES]
            acc = ws[SUBLANES:] * w[CONV_K - 1:CONV_K]
            for j in range(CONV_K - 1):
                acc = acc + pltpu.roll(ws, CONV_K - 1 - j, axis=0)[SUBLANES:] * w[j:j + 1]
            y = _silu(acc)
            if kind == 0:
                y = y * (lax.rsqrt(jnp.sum(y * y, axis=-1, keepdims=True) + EPS) * scale)
            elif kind == 1:
                y = y * lax.rsqrt(jnp.sum(y * y, axis=-1, keepdims=True) + EPS)
            o_ref[:, a + lo:a + lo + LANES] = y

    half = DN_HEADS // 2
    conv_parts = [(kind, range(p * half, (p + 1) * half)) for kind in range(3) for p in range(2)]
    rest = list(range(3 * DN_WIDTH, COL_BD * LANES, D_MODEL))
    for n in range(max(len(conv_parts), len(rest))):
        if n < len(rest):
            a = rest[n]
            o_ref[:, a:a + D_MODEL] = jnp.dot(hb, w_ref[:, a:a + D_MODEL], preferred_element_type=F32)
        if n < len(conv_parts):
            conv_heads(*conv_parts[n])

    bd = jnp.dot(hb, w_ref[:, COL_BD * LANES:], preferred_element_type=F32)
    g = -jnp.exp(gp_ref[0:1, :]) * _softplus(bd + gp_ref[1:2, :])
    row = lax.broadcasted_iota(jnp.int32, (tm, tm), 0)
    col = lax.broadcasted_iota(jnp.int32, (tm, tm), 1)
    shift = CHUNK.bit_length() - 1
    tril = (((row >> shift) == (col >> shift)) & (row >= col)).astype(F32)
    g_cum = _mm_f32(tril, g)
    lane = lax.broadcasted_iota(jnp.int32, (tm, LANES), 1)
    o_ref[:, COL_BD * LANES:] = jnp.where(lane < DECAY_LANE0, _sigmoid(bd), g_cum)


def _in_proj(x2, gain, w, conv_w8, gate_p, li, seq, tm):
    t = x2.shape[0]
    return pl.pallas_call(
        functools.partial(_in_proj_kernel, tm=tm, tiles_per_seq=seq // tm),
        grid=(t // tm,),
        in_specs=[
            pl.BlockSpec((tm, D_MODEL), lambda i: (i, 0)),
            pl.BlockSpec((None, 1, D_MODEL), lambda i: (li, 0, 0)),
            pl.BlockSpec((None, D_MODEL, PROJ_W), lambda i: (li, 0, 0), pipeline_mode=pl.Buffered(1)),
            pl.BlockSpec((None, SUBLANES, 3 * DN_WIDTH), lambda i: (li, 0, 0)),
            pl.BlockSpec((None, SUBLANES, LANES), lambda i: (li, 0, 0)),
        ],
        out_specs=pl.BlockSpec((tm, PROJ_W), lambda i: (i, 0)),
        out_shape=jax.ShapeDtypeStruct((t, PROJ_W), F32),
        scratch_shapes=[pltpu.VMEM((3, SUBLANES, DN_WIDTH), F32)],
        compiler_params=pltpu.CompilerParams(
            dimension_semantics=("arbitrary",), vmem_limit_bytes=VMEM_LIMIT),
        name="in_proj",
    )(x2, gain, w, conv_w8, gate_p)


def _neumann_inverse(l_mats, block):
    n = l_mats[0].shape[0]
    row = lax.broadcasted_iota(jnp.int32, (n, n), 0)
    col = lax.broadcasted_iota(jnp.int32, (n, n), 1)
    eye = jnp.where(row == col, 1.0, 0.0)
    ms = [-l for l in l_mats]
    ps = [eye + m for m in ms]
    for _ in range((block - 1).bit_length() - 1):
        mbs = [m.astype(BF16) for m in ms]
        ms = [jnp.dot(mb, mb, preferred_element_type=F32) for mb in mbs]
        ps = [p + _mm(p, m) for p, m in zip(ps, ms)]
    return ps


def _dn_kernel(q_ref, k_ref, v_ref, bd_ref, hn_ref, o_ref, state, *, ts, bt, heads):
    hb = pl.program_id(1)
    t = pl.program_id(2)
    nc = bt // CHUNK

    @pl.when(t == 0)
    def _():
        state[...] = jnp.zeros_like(state)

    head_gain = hn_ref[0:1, :]
    lane = lax.broadcasted_iota(jnp.int32, (bt, LANES), 1)
    sub = lax.broadcasted_iota(jnp.int32, (LANES, bt), 0)
    row = lax.broadcasted_iota(jnp.int32, (bt, bt), 0)
    col = lax.broadcasted_iota(jnp.int32, (bt, bt), 1)
    shift = CHUNK.bit_length() - 1
    same = (row >> shift) == (col >> shift)
    causal = same & (row >= col)
    strict = same & (row > col)

    units = [(st, hl) for st in range(ts // bt) for hl in range(heads)]
    gates = []
    for st in range(ts // bt):
        bd = bd_ref[st * bt:(st + 1) * bt, :]
        gates.append((bd, jnp.transpose(bd)))

    qn, kn, vv, beta, g_col, decay = [], [], [], [], [], []
    for st, hl in units:
        tok = slice(st * bt, (st + 1) * bt)
        qn.append(q_ref[tok, hl * LANES:(hl + 1) * LANES])
        kn.append(k_ref[tok, hl * LANES:(hl + 1) * LANES])
        vv.append(v_ref[tok, hl * LANES:(hl + 1) * LANES])
        hg = hb * heads + hl
        bd, bd_t = gates[st]
        beta.append(jnp.sum(jnp.where(lane == hg, bd, 0.0), axis=-1, keepdims=True))
        gc = jnp.sum(jnp.where(lane == hg + DECAY_LANE0, bd, 0.0), axis=-1, keepdims=True)
        gr = jnp.sum(jnp.where(sub == hg + DECAY_LANE0, bd_t, 0.0), axis=0, keepdims=True)
        g_col.append(gc)
        decay.append(jnp.exp(jnp.where(causal, gc - gr, -jnp.inf)))

    kb = [k * b for k, b in zip(kn, beta)]
    knb = [k.astype(BF16) for k in kn]
    l_mats = [jnp.where(strict, _mm_nt(a, b) * d, 0.0) for a, b, d in zip(kb, knb, decay)]
    a_qk = [_mm_nt(a, b) * d for a, b, d in zip(qn, knb, decay)]
    t_inv = _neumann_inverse(l_mats, CHUNK)
    e_g = [jnp.exp(g) for g in g_col]
    sol = [_mm(ti, jnp.concatenate([v * b, k * e], axis=-1))
           for ti, v, b, k, e in zip(t_inv, vv, beta, kb, e_g)]
    a_sol = [_mm(a, s) for a, s in zip(a_qk, sol)]
    q_eff = [q * e - a[:, DN_HEAD_DIM:] for q, e, a in zip(qn, e_g, a_sol)]
    g_last = [jnp.concatenate(
        [jnp.broadcast_to(g[(c + 1) * CHUNK - 1:(c + 1) * CHUNK, :], (CHUNK, 1)) for c in range(nc)], axis=0)
        for g in g_col]
    k_dec = [k * jnp.exp(gl - g) for k, gl, g in zip(kn, g_last, g_col)]
    gamma = [jnp.exp(gl) for gl in g_last]
    kt_sol = [[_mm_tn(kd[c * CHUNK:(c + 1) * CHUNK, :], s[c * CHUNK:(c + 1) * CHUNK, :]) for c in range(nc)]
              for kd, s in zip(k_dec, sol)]

    s_list = [state[hl] for hl in range(heads)]
    for st in range(ts // bt):
        for c in range(nc):
            rows = slice(c * CHUNK, (c + 1) * CHUNK)
            for hl in range(heads):
                i = st * heads + hl
                s = s_list[hl]
                o = _mm(q_eff[i][rows, :], s) + a_sol[i][rows, :DN_HEAD_DIM]
                s_list[hl] = (s * gamma[i][c * CHUNK:c * CHUNK + 1, :] - _mm(kt_sol[i][c][:, DN_HEAD_DIM:], s)
                              + kt_sol[i][c][:, :DN_HEAD_DIM])
                o = o * lax.rsqrt(jnp.mean(o * o, axis=-1, keepdims=True) + EPS) * head_gain
                o_ref[st * bt + c * CHUNK:st * bt + (c + 1) * CHUNK, hl * LANES:(hl + 1) * LANES] = o

    for hl in range(heads):
        state[hl] = s_list[hl]


def _deltanet(proj, head_n, li, bsz, seq, ts, bt, heads):
    nt = seq // ts
    wid = heads * LANES
    nqb = DN_WIDTH // wid
    tok = lambda b, h, t: b * nt + t
    return pl.pallas_call(
        functools.partial(_dn_kernel, ts=ts, bt=bt, heads=heads),
        grid=(bsz, DN_HEADS // heads, nt),
        in_specs=[
            pl.BlockSpec((ts, wid), lambda b, h, t: (tok(b, h, t), h)),
            pl.BlockSpec((ts, wid), lambda b, h, t: (tok(b, h, t), nqb + h)),
            pl.BlockSpec((ts, wid), lambda b, h, t: (tok(b, h, t), 2 * nqb + h)),
            pl.BlockSpec((ts, LANES), lambda b, h, t: (tok(b, h, t), COL_BD)),
            pl.BlockSpec((None, SUBLANES, LANES), lambda b, h, t: (li, 0, 0)),
        ],
        out_specs=pl.BlockSpec((ts, wid), lambda b, h, t: (tok(b, h, t), h)),
        out_shape=jax.ShapeDtypeStruct((bsz * seq, DN_WIDTH), F32),
        scratch_shapes=[pltpu.VMEM((heads, DN_HEAD_DIM, DN_HEAD_DIM), F32)],
        compiler_params=pltpu.CompilerParams(
            dimension_semantics=("arbitrary", "arbitrary", "arbitrary"), vmem_limit_bytes=VMEM_LIMIT),
        name="deltanet",
    )(proj, proj, proj, proj, head_n)


def _s5_param_kernel(are, aim, ldt, arer, aimr, ldtr, brt, bit, pwr, pwi, sqr, sqi, nsqi, bbr, bbi):
    ar = are[0]
    ai = aim[0]
    dt = jnp.exp(ldt[0])
    for k in range(S5_L + 1):
        mag = jnp.exp(ar * dt * float(k))
        ang = ai * dt * float(k)
        pwr[0, k] = mag * jnp.cos(ang)
        pwi[0, k] = mag * jnp.sin(ang)
    pr = pwr[0, S5_L]
    pi = pwi[0, S5_L]
    for j in range(S5_NPOW):
        sqr[0, j] = pr
        sqi[0, j] = pi
        nsqi[0, j] = -pi
        pr, pi = pr * pr - pi * pi, 2.0 * (pr * pi)
    ar = arer[0]
    ai = aimr[0]
    dt = jnp.exp(ldtr[0])
    mag = jnp.exp(ar * dt)
    lr = mag * jnp.cos(ai * dt)
    li = mag * jnp.sin(ai * dt)
    den = ar * ar + ai * ai
    fr = ((lr - 1.0) * ar + li * ai) / den
    fi = (li * ar - (lr - 1.0) * ai) / den
    br = brt[0]
    bi = bit[0]
    bbr[0] = fr * br - fi * bi
    bbi[0] = fr * bi + fi * br


def _s5_params(a_re, a_im, log_dt, b_re, b_im):
    nl = a_re.shape[0]
    g, n, c = SSM_GROUPS, SSM_STATE, SSM_GROUP
    ldt = jnp.broadcast_to(log_dt[..., None], (nl, g, n))
    rep = lambda a: jnp.repeat(a, c, axis=1)
    brt = jnp.swapaxes(b_re, 2, 3).reshape(nl, g * c, n)
    bit = jnp.swapaxes(b_im, 2, 3).reshape(nl, g * c, n)
    small = pl.BlockSpec((1, g, n), lambda l: (l, 0, 0))
    big = pl.BlockSpec((1, g * c, n), lambda l: (l, 0, 0))
    powr = pl.BlockSpec((1, S5_L + 1, g, n), lambda l: (l, 0, 0, 0))
    sqs = pl.BlockSpec((1, S5_NPOW, g, n), lambda l: (l, 0, 0, 0))
    pw_shape = jax.ShapeDtypeStruct((nl, S5_L + 1, g, n), F32)
    sq_shape = jax.ShapeDtypeStruct((nl, S5_NPOW, g, n), F32)
    bb_shape = jax.ShapeDtypeStruct((nl, g * c, n), F32)
    return pl.pallas_call(
        _s5_param_kernel,
        grid=(nl,),
        in_specs=[small, small, small, big, big, big, big, big],
        out_specs=[powr, powr, sqs, sqs, sqs, big, big],
        out_shape=[pw_shape, pw_shape, sq_shape, sq_shape, sq_shape, bb_shape, bb_shape],
        name="s5_params",
    )(a_re, a_im, ldt, rep(a_re), rep(a_im), rep(ldt), brt, bit)


def _s5_toeplitz_kernel(pwr, pwi, btr, bti, cr_ref, ci_ref, kt, vr, nvi, wr, wi):
    shape = (SSM_GROUP, SSM_STATE)
    for gl in range(S5_GB):
        pr = pwr[gl]
        pi = pwi[gl]
        rows = lambda p, ms: jnp.concatenate([jnp.broadcast_to(p[m:m + 1, :], shape) for m in ms], axis=0)
        tile = lambda a: jnp.concatenate([a] * S5_L, axis=0)
        cr, ci = tile(cr_ref[gl]), tile(ci_ref[gl])
        br, bi = tile(btr[gl]), tile(bti[gl])
        lr, li = rows(pr, range(S5_L)), rows(pi, range(S5_L))
        qr = cr * lr - ci * li
        qi = cr * li + ci * lr
        kt[gl] = _mm_nt_f32(qr, btr[gl]) - _mm_nt_f32(qi, bti[gl])
        lr, li = rows(pr, range(1, S5_L + 1)), rows(pi, range(1, S5_L + 1))
        vr[gl] = cr * lr - ci * li
        nvi[gl] = -(cr * li + ci * lr)
        lr, li = rows(pr, range(S5_L - 1, -1, -1)), rows(pi, range(S5_L - 1, -1, -1))
        wr[gl] = lr * br - li * bi
        wi[gl] = lr * bi + li * br


def _s5_weights(pwr, pwi, sqr, sqi, nsqi, bbr, bbi, c_re, c_im):
    nl = pwr.shape[0]
    g, n, c, ln = SSM_GROUPS, SSM_STATE, SSM_GROUP, S5_L
    by_group = lambda p: jnp.transpose(p, (0, 2, 1, 3))
    btr, bti = bbr.reshape(nl, g, c, n), bbi.reshape(nl, g, c, n)
    spec = lambda a, b: pl.BlockSpec((None, S5_GB, a, b), lambda l, q: (l, q, 0, 0))
    shp = lambda a, b: jax.ShapeDtypeStruct((nl, g, a, b), F32)
    kt, vr, nvi, wr, wi = pl.pallas_call(
        _s5_toeplitz_kernel,
        grid=(nl, g // S5_GB),
        in_specs=[spec(ln + 1, n)] * 2 + [spec(c, n)] * 4,
        out_specs=[spec(S5_LC, c)] + [spec(S5_LC, n)] * 4,
        out_shape=[shp(S5_LC, c)] + [shp(S5_LC, n)] * 4,
        name="s5_toeplitz",
    )(by_group(pwr), by_group(pwi), btr, bti, c_re, c_im)

    kall = jnp.transpose(kt.reshape(nl, g, ln, c, c), (0, 1, 4, 2, 3)).reshape(nl, g, c, S5_LC)
    rows = [jnp.pad(kall[..., :S5_LC - c * s], ((0, 0), (0, 0), (0, 0), (c * s, 0))) for s in range(ln)]
    toep = jnp.stack(rows, axis=2).reshape(nl, g, S5_LC, S5_LC)
    tw = jnp.concatenate([toep, wr, wi], axis=-1).astype(BF16)
    vt = jnp.concatenate([vr, nvi], axis=-1).astype(BF16)
    pair = lambda a, b: jnp.concatenate([by_group(a), by_group(b)], axis=-1)
    sc = jnp.concatenate([pair(sqr, sqr), pair(nsqi, sqi)], axis=-1)
    return tw, vt, sc


def _granule_transpose(xs):
    xs = list(xs)
    lane = lax.broadcasted_iota(jnp.int32, xs[0].shape, 1)
    gshift = SSM_GROUP.bit_length() - 1
    for m in range(S5_GB.bit_length() - 1):
        d = SSM_GROUP << m
        upper = ((lane >> (gshift + m)) & 1) == 1
        nxt = list(xs)
        for ia in range(S5_GB):
            if (ia >> m) & 1:
                continue
            ib = ia | (1 << m)
            a, b = xs[ia], xs[ib]
            nxt[ia] = jnp.where(upper, pltpu.roll(b, d, axis=1), a)
            nxt[ib] = jnp.where(upper, b, pltpu.roll(a, LANES - d, axis=1))
        xs = nxt
    return xs


def _s5_kernel(u_ref, tw_ref, v_ref, sc_ref, d_ref, y_ref, carry, *, ts):
    t = pl.program_id(1)
    i = pl.program_id(2)
    r = ts // S5_L
    half = SSM_STATE

    @pl.when(t == 0)
    def _():
        carry[pl.ds(i * S5_GB, S5_GB)] = jnp.zeros((S5_GB, SUBLANES, LANES), F32)

    a_rows = [u_ref[pl.ds(s, r, stride=S5_L), :] for s in range(S5_L)]
    folded = [_granule_transpose(a_rows[h * S5_GB:(h + 1) * S5_GB]) for h in range(S5_L // S5_GB)]
    row = lax.broadcasted_iota(jnp.int32, (r, LANES), 0)
    row8 = lax.broadcasted_iota(jnp.int32, (SUBLANES, LANES), 0)
    y_groups = []
    for gl in range(S5_GB):
        g = i * S5_GB + gl
        u_g = jnp.concatenate([f[gl] for f in folded], axis=1).astype(BF16)
        tx = jnp.dot(u_g, tw_ref[gl], preferred_element_type=F32)
        x = tx[:, S5_LC:]
        sc = sc_ref[gl]
        c8 = carry[g]
        inj = sc[0:1, :LANES] * c8 + sc[0:1, LANES:] * pltpu.roll(c8, half, axis=1)
        h = jnp.concatenate([x[:SUBLANES] + jnp.where(row8 == 0, inj, 0.0), x[SUBLANES:]], axis=0)
        for j in range(r.bit_length() - 1):
            sh = 1 << j
            hs = jnp.where(row >= sh, pltpu.roll(h, sh, axis=0), 0.0)
            h = h + sc[j:j + 1, :LANES] * hs + sc[j:j + 1, LANES:] * pltpu.roll(hs, half, axis=1)
        h_prev = jnp.where(row >= 1, pltpu.roll(h, 1, axis=0), jnp.broadcast_to(c8[0:1], (r, LANES)))
        y_groups.append(tx[:, :S5_LC] + _mm_nt(h_prev, v_ref[gl]))
        carry[g] = jnp.broadcast_to(h[r - 1:r], (SUBLANES, LANES))
    d_skip = d_ref[...]
    for h in range(S5_L // S5_GB):
        unfolded = _granule_transpose([y[:, h * LANES:(h + 1) * LANES] for y in y_groups])
        for q in range(S5_GB):
            s = h * S5_GB + q
            y_ref[pl.ds(s, r, stride=S5_L), :] = unfolded[q] + d_skip * a_rows[s]


def _s5(proj, tw, v, sc, d_skip, li, bsz, seq, ts):
    nt = seq // ts
    ncb = D_MODEL // LANES
    grp = lambda a, b: pl.BlockSpec((None, S5_GB, a, b), lambda b_, t, i: (li, i, 0, 0))
    return pl.pallas_call(
        functools.partial(_s5_kernel, ts=ts),
        grid=(bsz, nt, ncb),
        in_specs=[
            pl.BlockSpec((ts, LANES), lambda b, t, i: (b * nt + t, COL_U * ncb + i)),
            grp(S5_LC, S5_LC + 2 * SSM_STATE),
            grp(S5_LC, 2 * SSM_STATE),
            grp(S5_NPOW, 2 * LANES),
            pl.BlockSpec((None, 1, LANES), lambda b, t, i: (li, 0, i)),
        ],
        out_specs=pl.BlockSpec((ts, LANES), lambda b, t, i: (b * nt + t, i)),
        out_shape=jax.ShapeDtypeStruct((bsz * seq, D_MODEL), F32),
        scratch_shapes=[pltpu.VMEM((SSM_GROUPS, SUBLANES, LANES), F32)],
        compiler_params=pltpu.CompilerParams(
            dimension_semantics=("arbitrary", "arbitrary", "arbitrary"), vmem_limit_bytes=VMEM_LIMIT),
        name="s5",
    )(proj, tw, v, sc, d_skip)


def _merge_kernel(ya_ref, za_ref, ys_ref, zb_ref, ra_ref, rb_ref, x_ref, wglu_ref, bglu_ref,
                  wout_ref, np_ref, o_ref):
    y_a = ya_ref[...] * _silu(za_ref[...])
    y = _gelu_tanh(ys_ref[...])
    y = y * _sigmoid(jnp.dot(y.astype(BF16), wglu_ref[...], preferred_element_type=F32) + bglu_ref[...])
    y_b = y * _silu(zb_ref[...])
    merged = _sigmoid(ra_ref[...]) * y_a + _sigmoid(rb_ref[...]) * y_b
    out = jnp.dot(merged.astype(BF16), wout_ref[...], preferred_element_type=F32)
    out = out * lax.rsqrt(jnp.mean(out * out, axis=-1, keepdims=True) + EPS) * np_ref[...]
    o_ref[...] = x_ref[...] + out


def _merge(y_a, y_s, proj, x2, w_glu, b_glu, w_out, norm_post, li, tm):
    t = x2.shape[0]
    col = lambda c: pl.BlockSpec((tm, D_MODEL), lambda i: (i, c))
    vec = pl.BlockSpec((None, 1, D_MODEL), lambda i: (li, 0, 0))
    mat = pl.BlockSpec((None, D_MODEL, D_MODEL), lambda i: (li, 0, 0))
    return pl.pallas_call(
        _merge_kernel,
        grid=(t // tm,),
        in_specs=[col(0), col(COL_ZA), col(0), col(COL_ZB), col(COL_RA), col(COL_RB), col(0),
                  mat, vec, mat, vec],
        out_specs=col(0),
        out_shape=jax.ShapeDtypeStruct((t, D_MODEL), F32),
        compiler_params=pltpu.CompilerParams(
            dimension_semantics=("arbitrary",), vmem_limit_bytes=VMEM_LIMIT),
        name="merge",
    )(y_a, proj, y_s, proj, proj, proj, x2, w_glu, b_glu, w_out, norm_post)


def _prep_w_in(w_in):
    w_in = w_in.astype(BF16)
    o = 3 * DN_WIDTH
    qkv, za = w_in[..., :o], w_in[..., o:o + DN_WIDTH]
    o += DN_WIDTH
    bd = w_in[..., o:o + 2 * DN_HEADS]
    o += 2 * DN_HEADS
    u, zb, ra, rb = (w_in[..., o + i * D_MODEL:o + (i + 1) * D_MODEL] for i in range(4))
    pad = jnp.zeros(w_in.shape[:-1] + (LANES - 2 * DN_HEADS,), w_in.dtype)
    return jnp.concatenate([qkv, za, u, zb, ra, rb, bd, pad], axis=-1).astype(BF16)


def _pad_rows(a, rows=SUBLANES):
    return jnp.pad(a, ((0, 0), (0, rows - a.shape[1]), (0, 0)))


def _trunk(x, norm_pre, w_in, conv_w, a_log, dt_bias, head_norm, ssm_a_re, ssm_a_im, ssm_log_dt,
           ssm_b_re, ssm_b_im, ssm_c_re, ssm_c_im, ssm_d, w_glu, b_glu, w_out, norm_post,
           *, tm=256, ts_dn=256, bt_dn=128, dn_heads=8, ts_s5=2048):
    bsz, seq, _ = x.shape
    depth = w_in.shape[0]
    ts_s5 = min(ts_s5, seq)
    x2 = x.reshape(bsz * seq, D_MODEL)
    w_in_r = _prep_w_in(w_in)
    w_glu_b = w_glu.astype(BF16)
    w_out_b = w_out.astype(BF16)
    s5_tw, s5_v, s5_sc = _s5_weights(*_s5_params(ssm_a_re, ssm_a_im, ssm_log_dt, ssm_b_re, ssm_b_im),
                                     ssm_c_re, ssm_c_im)
    at_decay = lambda p: jnp.pad(p, ((0, 0), (DECAY_LANE0, LANES - DECAY_LANE0 - DN_HEADS)))
    gate_p = _pad_rows(jnp.stack([at_decay(a_log), at_decay(dt_bias)], axis=1))
    conv_w8 = _pad_rows(conv_w)
    head_n = _pad_rows(head_norm[:, None, :])
    row = lambda p: p[:, None, :]
    for li in range(depth):
        proj = _in_proj(x2, row(norm_pre), w_in_r, conv_w8, gate_p, li, seq, tm)
        y_a = _deltanet(proj, head_n, li, bsz, seq, ts_dn, bt_dn, dn_heads)
        y_s = _s5(proj, s5_tw, s5_v, s5_sc, row(ssm_d), li, bsz, seq, ts_s5)
        x2 = _merge(y_a, y_s, proj, x2, w_glu_b, row(b_glu), w_out_b, row(norm_post), li, tm)
    return x2.reshape(bsz, seq, D_MODEL)


def kernel(x, norm_pre, w_in, conv_w, a_log, dt_bias, head_norm, ssm_a_re, ssm_a_im, ssm_log_dt,
           ssm_b_re, ssm_b_im, ssm_c_re, ssm_c_im, ssm_d, w_glu, b_glu, w_out, norm_post):
    return _trunk(x, norm_pre, w_in, conv_w, a_log, dt_bias, head_norm, ssm_a_re, ssm_a_im, ssm_log_dt,
                  ssm_b_re, ssm_b_im, ssm_c_re, ssm_c_im, ssm_d, w_glu, b_glu, w_out, norm_post)
```

```python
import functools

import jax
import jax.numpy as jnp
from jax import lax
from jax.experimental import pallas as pl
from jax.experimental.pallas import tpu as pltpu

D_MODEL = 1024
DN_HEADS = 8
DN_HEAD_DIM = 128
DN_WIDTH = DN_HEADS * DN_HEAD_DIM
CONV_K = 4
CHUNK = 64
SSM_GROUP = 16
SSM_GROUPS = 64
SSM_STATE = 64
EPS = 1e-6

LANES = 128
SUBLANES = 8
VMEM_LIMIT = 56 * 1024 * 1024

PROJ_A_W = 5 * D_MODEL
PROJ_B_W = 3 * D_MODEL + LANES
A_ZA, A_U = 3, 4
B_ZB, B_RA, B_RB = 0, 1, 2
PROJ_B_BD = 3 * D_MODEL // LANES
DECAY_LANE0 = DN_HEADS
GATE_BLOCK = 256

S5_L = 16
S5_LC = S5_L * SSM_GROUP
S5_GB = LANES // SSM_GROUP
S5_NPOW = 8

F32 = jnp.float32
BF16 = jnp.bfloat16
HIGHEST = lax.Precision.HIGHEST


def _sigmoid(x):
    return 0.5 * jnp.tanh(0.5 * x) + 0.5


def _silu(x):
    hx = 0.5 * x
    return hx * jnp.tanh(hx) + hx


def _gelu_tanh(x):
    c = 0.7978845608028654
    return 0.5 * x * (1.0 + jnp.tanh(c * (x + 0.044715 * (x * x * x))))


def _softplus(x):
    return jnp.maximum(x, 0.0) + jnp.log(1.0 + jnp.exp(-jnp.abs(x)))


def _mm(a, b):
    return jnp.dot(a.astype(BF16), b.astype(BF16), preferred_element_type=F32)


def _mm_nt(a, b):
    return lax.dot_general(a.astype(BF16), b.astype(BF16), (((1,), (1,)), ((), ())),
                           preferred_element_type=F32)


def _mm_tn(a, b):
    return lax.dot_general(a.astype(BF16), b.astype(BF16), (((0,), (0,)), ((), ())),
                           preferred_element_type=F32)


def _mm_f32(a, b):
    return jnp.dot(a, b, precision=HIGHEST, preferred_element_type=F32)


def _mm_nt_f32(a, b):
    return lax.dot_general(a, b, (((1,), (1,)), ((), ())), precision=HIGHEST,
                           preferred_element_type=F32)


def _in_proj_kernel(x_ref, g_ref, wa_ref, wb_ref, cw_ref, gp_ref, pa_ref, pb_ref, halo, *, tm, tiles_per_seq):
    i = pl.program_id(0)

    @pl.when(i % tiles_per_seq == 0)
    def _():
        halo[...] = jnp.zeros_like(halo)

    x = x_ref[...]
    h = x * lax.rsqrt(jnp.mean(x * x, axis=-1, keepdims=True) + EPS) * g_ref[...]
    hb = h.astype(BF16)
    bd = jnp.dot(hb, wb_ref[:, PROJ_B_BD * LANES:], preferred_element_type=F32)
    g = -jnp.exp(gp_ref[0:1, :]) * _softplus(bd + gp_ref[1:2, :])
    sb = min(tm, GATE_BLOCK)
    row = lax.broadcasted_iota(jnp.int32, (sb, sb), 0)
    col = lax.broadcasted_iota(jnp.int32, (sb, sb), 1)
    shift = CHUNK.bit_length() - 1
    tril = (((row >> shift) == (col >> shift)) & (row >= col)).astype(F32)
    g_cum = jnp.concatenate([_mm_f32(tril, g[r0:r0 + sb, :]) for r0 in range(0, tm, sb)], axis=0)
    lane = lax.broadcasted_iota(jnp.int32, (tm, LANES), 1)
    pb_ref[:, PROJ_B_BD * LANES:] = jnp.where(lane < DECAY_LANE0, _sigmoid(bd), g_cum)

    wins = []
    for kind in range(3):
        a = kind * DN_WIDTH
        raw = jnp.dot(hb, wa_ref[:, a:a + DN_WIDTH], preferred_element_type=F32)
        wins.append(jnp.concatenate([halo[kind], raw], axis=0))
        halo[kind] = raw[tm - SUBLANES:, :]
    for a in range(3 * DN_WIDTH, PROJ_A_W, D_MODEL):
        pa_ref[:, a:a + D_MODEL] = jnp.dot(hb, wa_ref[:, a:a + D_MODEL], preferred_element_type=F32)
    for a in range(0, PROJ_B_BD * LANES, D_MODEL):
        pb_ref[:, a:a + D_MODEL] = jnp.dot(hb, wb_ref[:, a:a + D_MODEL], preferred_element_type=F32)

    scale = DN_HEAD_DIM ** -0.5
    for kind in range(3):
        a = kind * DN_WIDTH
        ws = wins[kind]
        w = cw_ref[:, a:a + DN_WIDTH]
        acc = ws[SUBLANES:] * w[CONV_K - 1:CONV_K]
        for j in range(CONV_K - 1):
            acc = acc + pltpu.roll(ws, CONV_K - 1 - j, axis=0)[SUBLANES:] * w[j:j + 1]
        y = _silu(acc)
        for hl in range(DN_HEADS):
            lo = hl * LANES
            yh = y[:, lo:lo + LANES]
            if kind == 0:
                yh = yh * (lax.rsqrt(jnp.sum(yh * yh, axis=-1, keepdims=True) + EPS) * scale)
            elif kind == 1:
                yh = yh * lax.rsqrt(jnp.sum(yh * yh, axis=-1, keepdims=True) + EPS)
            pa_ref[:, a + lo:a + lo + LANES] = yh


def _in_proj(x2, gain, w_a, w_b, conv_w8, gate_p, li, seq, tm):
    t = x2.shape[0]
    return pl.pallas_call(
        functools.partial(_in_proj_kernel, tm=tm, tiles_per_seq=seq // tm),
        grid=(t // tm,),
        in_specs=[
            pl.BlockSpec((tm, D_MODEL), lambda i: (i, 0)),
            pl.BlockSpec((None, 1, D_MODEL), lambda i: (li, 0, 0)),
            pl.BlockSpec((None, D_MODEL, PROJ_A_W), lambda i: (li, 0, 0), pipeline_mode=pl.Buffered(1)),
            pl.BlockSpec((None, D_MODEL, PROJ_B_W), lambda i: (li, 0, 0), pipeline_mode=pl.Buffered(1)),
            pl.BlockSpec((None, SUBLANES, 3 * DN_WIDTH), lambda i: (li, 0, 0)),
            pl.BlockSpec((None, SUBLANES, LANES), lambda i: (li, 0, 0)),
        ],
        out_specs=[
            pl.BlockSpec((tm, PROJ_A_W), lambda i: (i, 0)),
            pl.BlockSpec((tm, PROJ_B_W), lambda i: (i, 0)),
        ],
        out_shape=[
            jax.ShapeDtypeStruct((t, PROJ_A_W), F32),
            jax.ShapeDtypeStruct((t, PROJ_B_W), F32),
        ],
        scratch_shapes=[pltpu.VMEM((3, SUBLANES, DN_WIDTH), F32)],
        compiler_params=pltpu.CompilerParams(
            dimension_semantics=("arbitrary",), vmem_limit_bytes=VMEM_LIMIT),
        name="in_proj",
    )(x2, gain, w_a, w_b, conv_w8, gate_p)


def _neumann_inverse(l_mats, block):
    n = l_mats[0].shape[0]
    row = lax.broadcasted_iota(jnp.int32, (n, n), 0)
    col = lax.broadcasted_iota(jnp.int32, (n, n), 1)
    eye = jnp.where(row == col, 1.0, 0.0)
    ms = [-l for l in l_mats]
    ps = [eye + m for m in ms]
    for _ in range((block - 1).bit_length() - 1):
        mbs = [m.astype(BF16) for m in ms]
        ms = [jnp.dot(mb, mb, preferred_element_type=F32) for mb in mbs]
        ps = [p + _mm(p, m) for p, m in zip(ps, ms)]
    return ps


def _dn_kernel(q_ref, k_ref, v_ref, bd_ref, hn_ref, o_ref, state, *, ts, bt, heads):
    hb = pl.program_id(1)
    t = pl.program_id(2)
    nc = bt // CHUNK

    @pl.when(t == 0)
    def _():
        state[...] = jnp.zeros_like(state)

    head_gain = hn_ref[0:1, :]
    lane = lax.broadcasted_iota(jnp.int32, (bt, LANES), 1)
    sub = lax.broadcasted_iota(jnp.int32, (LANES, bt), 0)
    row = lax.broadcasted_iota(jnp.int32, (bt, bt), 0)
    col = lax.broadcasted_iota(jnp.int32, (bt, bt), 1)
    shift = CHUNK.bit_length() - 1
    same = (row >> shift) == (col >> shift)
    causal = same & (row >= col)
    strict = same & (row > col)

    units = [(st, hl) for st in range(ts // bt) for hl in range(heads)]
    gates = []
    for st in range(ts // bt):
        bd = bd_ref[st * bt:(st + 1) * bt, :]
        gates.append((bd, jnp.transpose(bd)))

    qn, kn, vv, beta, g_col, decay = [], [], [], [], [], []
    for st, hl in units:
        tok = slice(st * bt, (st + 1) * bt)
        qn.append(q_ref[tok, hl * LANES:(hl + 1) * LANES])
        kn.append(k_ref[tok, hl * LANES:(hl + 1) * LANES])
        vv.append(v_ref[tok, hl * LANES:(hl + 1) * LANES])
        hg = hb * heads + hl
        bd, bd_t = gates[st]
        beta.append(jnp.sum(jnp.where(lane == hg, bd, 0.0), axis=-1, keepdims=True))
        gc = jnp.sum(jnp.where(lane == hg + DECAY_LANE0, bd, 0.0), axis=-1, keepdims=True)
        gr = jnp.sum(jnp.where(sub == hg + DECAY_LANE0, bd_t, 0.0), axis=0, keepdims=True)
        g_col.append(gc)
        decay.append(jnp.exp(jnp.where(causal, gc - gr, -jnp.inf)))

    kb = [k * b for k, b in zip(kn, beta)]
    knb = [k.astype(BF16) for k in kn]
    l_mats = [jnp.where(strict, _mm_nt(a, b) * d, 0.0) for a, b, d in zip(kb, knb, decay)]
    a_qk = [_mm_nt(a, b) * d for a, b, d in zip(qn, knb, decay)]
    t_inv = _neumann_inverse(l_mats, CHUNK)
    e_g = [jnp.exp(g) for g in g_col]
    sol = [_mm(ti, jnp.concatenate([v * b, k * e], axis=-1))
           for ti, v, b, k, e in zip(t_inv, vv, beta, kb, e_g)]
    a_sol = [_mm(a, s) for a, s in zip(a_qk, sol)]
    q_eff = [q * e - a[:, DN_HEAD_DIM:] for q, e, a in zip(qn, e_g, a_sol)]
    g_last = [jnp.concatenate(
        [jnp.broadcast_to(g[(c + 1) * CHUNK - 1:(c + 1) * CHUNK, :], (CHUNK, 1)) for c in range(nc)], axis=0)
        for g in g_col]
    k_dec = [k * jnp.exp(gl - g) for k, gl, g in zip(kn, g_last, g_col)]
    gamma = [jnp.exp(gl) for gl in g_last]
    kt_sol = [[_mm_tn(kd[c * CHUNK:(c + 1) * CHUNK, :], s[c * CHUNK:(c + 1) * CHUNK, :]) for c in range(nc)]
              for kd, s in zip(k_dec, sol)]

    s_list = [state[hl] for hl in range(heads)]
    for st in range(ts // bt):
        for c in range(nc):
            rows = slice(c * CHUNK, (c + 1) * CHUNK)
            for hl in range(heads):
                i = st * heads + hl
                s = s_list[hl]
                o = _mm(q_eff[i][rows, :], s) + a_sol[i][rows, :DN_HEAD_DIM]
                s_list[hl] = (s * gamma[i][c * CHUNK:c * CHUNK + 1, :] - _mm(kt_sol[i][c][:, DN_HEAD_DIM:], s)
                              + kt_sol[i][c][:, :DN_HEAD_DIM])
                o = o * lax.rsqrt(jnp.mean(o * o, axis=-1, keepdims=True) + EPS) * head_gain
                o_ref[st * bt + c * CHUNK:st * bt + (c + 1) * CHUNK, hl * LANES:(hl + 1) * LANES] = o

    for hl in range(heads):
        state[hl] = s_list[hl]


def _deltanet(proj_a, proj_b, head_n, li, bsz, seq, ts, bt, heads):
    nt = seq // ts
    wid = heads * LANES
    nqb = DN_WIDTH // wid
    tok = lambda b, h, t: b * nt + t
    return pl.pallas_call(
        functools.partial(_dn_kernel, ts=ts, bt=bt, heads=heads),
        grid=(bsz, DN_HEADS // heads, nt),
        in_specs=[
            pl.BlockSpec((ts, wid), lambda b, h, t: (tok(b, h, t), h)),
            pl.BlockSpec((ts, wid), lambda b, h, t: (tok(b, h, t), nqb + h)),
            pl.BlockSpec((ts, wid), lambda b, h, t: (tok(b, h, t), 2 * nqb + h)),
            pl.BlockSpec((ts, LANES), lambda b, h, t: (tok(b, h, t), PROJ_B_BD)),
            pl.BlockSpec((None, SUBLANES, LANES), lambda b, h, t: (li, 0, 0)),
        ],
        out_specs=pl.BlockSpec((ts, wid), lambda b, h, t: (tok(b, h, t), h)),
        out_shape=jax.ShapeDtypeStruct((bsz * seq, DN_WIDTH), F32),
        scratch_shapes=[pltpu.VMEM((heads, DN_HEAD_DIM, DN_HEAD_DIM), F32)],
        compiler_params=pltpu.CompilerParams(
            dimension_semantics=("arbitrary", "arbitrary", "arbitrary"), vmem_limit_bytes=VMEM_LIMIT),
        name="deltanet",
    )(proj_a, proj_a, proj_a, proj_b, head_n)


def _s5_param_kernel(are, aim, ldt, arer, aimr, ldtr, brt, bit, pwr, pwi, sqr, sqi, nsqi, bbr, bbi):
    ar = are[0]
    ai = aim[0]
    dt = jnp.exp(ldt[0])
    for k in range(S5_L + 1):
        mag = jnp.exp(ar * dt * float(k))
        ang = ai * dt * float(k)
        pwr[0, k] = mag * jnp.cos(ang)
        pwi[0, k] = mag * jnp.sin(ang)
    pr = pwr[0, S5_L]
    pi = pwi[0, S5_L]
    for j in range(S5_NPOW):
        sqr[0, j] = pr
        sqi[0, j] = pi
        nsqi[0, j] = -pi
        pr, pi = pr * pr - pi * pi, 2.0 * (pr * pi)
    ar = arer[0]
    ai = aimr[0]
    dt = jnp.exp(ldtr[0])
    mag = jnp.exp(ar * dt)
    lr = mag * jnp.cos(ai * dt)
    li = mag * jnp.sin(ai * dt)
    den = ar * ar + ai * ai
    fr = ((lr - 1.0) * ar + li * ai) / den
    fi = (li * ar - (lr - 1.0) * ai) / den
    br = brt[0]
    bi = bit[0]
    bbr[0] = fr * br - fi * bi
    bbi[0] = fr * bi + fi * br


def _s5_params(a_re, a_im, log_dt, b_re, b_im):
    nl = a_re.shape[0]
    g, n, c = SSM_GROUPS, SSM_STATE, SSM_GROUP
    ldt = jnp.broadcast_to(log_dt[..., None], (nl, g, n))
    rep = lambda a: jnp.repeat(a, c, axis=1)
    brt = jnp.swapaxes(b_re, 2, 3).reshape(nl, g * c, n)
    bit = jnp.swapaxes(b_im, 2, 3).reshape(nl, g * c, n)
    small = pl.BlockSpec((1, g, n), lambda l: (l, 0, 0))
    big = pl.BlockSpec((1, g * c, n), lambda l: (l, 0, 0))
    powr = pl.BlockSpec((1, S5_L + 1, g, n), lambda l: (l, 0, 0, 0))
    sqs = pl.BlockSpec((1, S5_NPOW, g, n), lambda l: (l, 0, 0, 0))
    pw_shape = jax.ShapeDtypeStruct((nl, S5_L + 1, g, n), F32)
    sq_shape = jax.ShapeDtypeStruct((nl, S5_NPOW, g, n), F32)
    bb_shape = jax.ShapeDtypeStruct((nl, g * c, n), F32)
    return pl.pallas_call(
        _s5_param_kernel,
        grid=(nl,),
        in_specs=[small, small, small, big, big, big, big, big],
        out_specs=[powr, powr, sqs, sqs, sqs, big, big],
        out_shape=[pw_shape, pw_shape, sq_shape, sq_shape, sq_shape, bb_shape, bb_shape],
        name="s5_params",
    )(a_re, a_im, ldt, rep(a_re), rep(a_im), rep(ldt), brt, bit)


def _s5_toeplitz_kernel(pwr, pwi, btr, bti, cr_ref, ci_ref, kt, vr, nvi, wr, wi):
    shape = (SSM_GROUP, SSM_STATE)
    for gl in range(S5_GB):
        pr = pwr[gl]
        pi = pwi[gl]
        rows = lambda p, ms: jnp.concatenate([jnp.broadcast_to(p[m:m + 1, :], shape) for m in ms], axis=0)
        tile = lambda a: jnp.concatenate([a] * S5_L, axis=0)
        cr, ci = tile(cr_ref[gl]), tile(ci_ref[gl])
        br, bi = tile(btr[gl]), tile(bti[gl])
        lr, li = rows(pr, range(S5_L)), rows(pi, range(S5_L))
        qr = cr * lr - ci * li
        qi = cr * li + ci * lr
        kt[gl] = _mm_nt_f32(qr, btr[gl]) - _mm_nt_f32(qi, bti[gl])
        lr, li = rows(pr, range(1, S5_L + 1)), rows(pi, range(1, S5_L + 1))
        vr[gl] = cr * lr - ci * li
        nvi[gl] = -(cr * li + ci * lr)
        lr, li = rows(pr, range(S5_L - 1, -1, -1)), rows(pi, range(S5_L - 1, -1, -1))
        wr[gl] = lr * br - li * bi
        wi[gl] = lr * bi + li * br


def _s5_weights(pwr, pwi, sqr, sqi, nsqi, bbr, bbi, c_re, c_im):
    nl = pwr.shape[0]
    g, n, c, ln = SSM_GROUPS, SSM_STATE, SSM_GROUP, S5_L
    by_group = lambda p: jnp.transpose(p, (0, 2, 1, 3))
    btr, bti = bbr.reshape(nl, g, c, n), bbi.reshape(nl, g, c, n)
    spec = lambda a, b: pl.BlockSpec((None, S5_GB, a, b), lambda l, q: (l, q, 0, 0))
    shp = lambda a, b: jax.ShapeDtypeStruct((nl, g, a, b), F32)
    kt, vr, nvi, wr, wi = pl.pallas_call(
        _s5_toeplitz_kernel,
        grid=(nl, g // S5_GB),
        in_specs=[spec(ln + 1, n)] * 2 + [spec(c, n)] * 4,
        out_specs=[spec(S5_LC, c)] + [spec(S5_LC, n)] * 4,
        out_shape=[shp(S5_LC, c)] + [shp(S5_LC, n)] * 4,
        name="s5_toeplitz",
    )(by_group(pwr), by_group(pwi), btr, bti, c_re, c_im)

    kall = jnp.transpose(kt.reshape(nl, g, ln, c, c), (0, 1, 4, 2, 3)).reshape(nl, g, c, S5_LC)
    rows = [jnp.pad(kall[..., :S5_LC - c * s], ((0, 0), (0, 0), (0, 0), (c * s, 0))) for s in range(ln)]
    toep = jnp.stack(rows, axis=2).reshape(nl, g, S5_LC, S5_LC)
    tw = jnp.concatenate([toep, wr, wi], axis=-1).astype(BF16)
    vt = jnp.concatenate([vr, nvi], axis=-1).astype(BF16)
    pair = lambda a, b: jnp.concatenate([by_group(a), by_group(b)], axis=-1)
    sc = jnp.concatenate([pair(sqr, sqr), pair(nsqi, sqi)], axis=-1)
    return tw, vt, sc


def _granule_transpose(xs):
    xs = list(xs)
    lane = lax.broadcasted_iota(jnp.int32, xs[0].shape, 1)
    gshift = SSM_GROUP.bit_length() - 1
    for m in range(S5_GB.bit_length() - 1):
        d = SSM_GROUP << m
        upper = ((lane >> (gshift + m)) & 1) == 1
        nxt = list(xs)
        for ia in range(S5_GB):
            if (ia >> m) & 1:
                continue
            ib = ia | (1 << m)
            a, b = xs[ia], xs[ib]
            nxt[ia] = jnp.where(upper, pltpu.roll(b, d, axis=1), a)
            nxt[ib] = jnp.where(upper, b, pltpu.roll(a, LANES - d, axis=1))
        xs = nxt
    return xs


def _s5_kernel(u_ref, tw_ref, v_ref, sc_ref, d_ref, y_ref, carry, *, ts):
    t = pl.program_id(1)
    i = pl.program_id(2)
    r = ts // S5_L
    half = SSM_STATE

    @pl.when(t == 0)
    def _():
        carry[pl.ds(i * S5_GB, S5_GB)] = jnp.zeros((S5_GB, SUBLANES, LANES), F32)

    a_rows = [u_ref[pl.ds(s, r, stride=S5_L), :] for s in range(S5_L)]
    folded = [_granule_transpose(a_rows[h * S5_GB:(h + 1) * S5_GB]) for h in range(S5_L // S5_GB)]
    row = lax.broadcasted_iota(jnp.int32, (r, LANES), 0)
    row8 = lax.broadcasted_iota(jnp.int32, (SUBLANES, LANES), 0)
    groups = range(S5_GB)
    txs = [jnp.dot(jnp.concatenate([f[gl] for f in folded], axis=1).astype(BF16), tw_ref[gl],
                   preferred_element_type=F32) for gl in groups]
    scs = [sc_ref[gl] for gl in groups]
    c8s = [carry[i * S5_GB + gl] for gl in groups]
    hs_ = []
    for tx, sc, c8 in zip(txs, scs, c8s):
        x = tx[:, S5_LC:]
        inj = sc[0:1, :LANES] * c8 + sc[0:1, LANES:] * pltpu.roll(c8, half, axis=1)
        hs_.append(jnp.concatenate([x[:SUBLANES] + jnp.where(row8 == 0, inj, 0.0), x[SUBLANES:]], axis=0))
    for j in range(r.bit_length() - 1):
        sh = 1 << j
        shifted = [jnp.where(row >= sh, pltpu.roll(h, sh, axis=0), 0.0) for h in hs_]
        hs_ = [h + sc[j:j + 1, :LANES] * s + sc[j:j + 1, LANES:] * pltpu.roll(s, half, axis=1)
               for h, s, sc in zip(hs_, shifted, scs)]
    y_groups = []
    for gl, (tx, h, c8) in enumerate(zip(txs, hs_, c8s)):
        h_prev = jnp.where(row >= 1, pltpu.roll(h, 1, axis=0), jnp.broadcast_to(c8[0:1], (r, LANES)))
        y_groups.append(tx[:, :S5_LC] + _mm_nt(h_prev, v_ref[gl]))
        carry[i * S5_GB + gl] = jnp.broadcast_to(h[r - 1:r], (SUBLANES, LANES))
    d_skip = d_ref[...]
    for h in range(S5_L // S5_GB):
        unfolded = _granule_transpose([y[:, h * LANES:(h + 1) * LANES] for y in y_groups])
        for q in range(S5_GB):
            s = h * S5_GB + q
            y_ref[pl.ds(s, r, stride=S5_L), :] = unfolded[q] + d_skip * a_rows[s]


def _s5(proj, tw, v, sc, d_skip, li, bsz, seq, ts):
    nt = seq // ts
    ncb = D_MODEL // LANES
    grp = lambda a, b: pl.BlockSpec((None, S5_GB, a, b), lambda b_, t, i: (li, i, 0, 0))
    return pl.pallas_call(
        functools.partial(_s5_kernel, ts=ts),
        grid=(bsz, nt, ncb),
        in_specs=[
            pl.BlockSpec((ts, LANES), lambda b, t, i: (b * nt + t, A_U * ncb + i)),
            grp(S5_LC, S5_LC + 2 * SSM_STATE),
            grp(S5_LC, 2 * SSM_STATE),
            grp(S5_NPOW, 2 * LANES),
            pl.BlockSpec((None, 1, LANES), lambda b, t, i: (li, 0, i)),
        ],
        out_specs=pl.BlockSpec((ts, LANES), lambda b, t, i: (b * nt + t, i)),
        out_shape=jax.ShapeDtypeStruct((bsz * seq, D_MODEL), F32),
        scratch_shapes=[pltpu.VMEM((SSM_GROUPS, SUBLANES, LANES), F32)],
        compiler_params=pltpu.CompilerParams(
            dimension_semantics=("arbitrary", "arbitrary", "arbitrary"), vmem_limit_bytes=VMEM_LIMIT),
        name="s5",
    )(proj, tw, v, sc, d_skip)


def _merge_kernel(ya_ref, za_ref, ys_ref, zb_ref, ra_ref, rb_ref, x_ref, wglu_ref, bglu_ref,
                  wout_ref, np_ref, o_ref):
    y_a = ya_ref[...] * _silu(za_ref[...])
    y = _gelu_tanh(ys_ref[...])
    y = y * _sigmoid(jnp.dot(y.astype(BF16), wglu_ref[...], preferred_element_type=F32) + bglu_ref[...])
    y_b = y * _silu(zb_ref[...])
    merged = _sigmoid(ra_ref[...]) * y_a + _sigmoid(rb_ref[...]) * y_b
    out = jnp.dot(merged.astype(BF16), wout_ref[...], preferred_element_type=F32)
    out = out * lax.rsqrt(jnp.mean(out * out, axis=-1, keepdims=True) + EPS) * np_ref[...]
    o_ref[...] = x_ref[...] + out


def _merge(y_a, y_s, proj_a, proj_b, x2, w_glu, b_glu, w_out, norm_post, li, tm):
    t = x2.shape[0]
    col = lambda c: pl.BlockSpec((tm, D_MODEL), lambda i: (i, c))
    vec = pl.BlockSpec((None, 1, D_MODEL), lambda i: (li, 0, 0))
    mat = pl.BlockSpec((None, D_MODEL, D_MODEL), lambda i: (li, 0, 0))
    return pl.pallas_call(
        _merge_kernel,
        grid=(t // tm,),
        in_specs=[col(0), col(A_ZA), col(0), col(B_ZB), col(B_RA), col(B_RB), col(0),
                  mat, vec, mat, vec],
        out_specs=col(0),
        out_shape=jax.ShapeDtypeStruct((t, D_MODEL), F32),
        compiler_params=pltpu.CompilerParams(
            dimension_semantics=("arbitrary",), vmem_limit_bytes=VMEM_LIMIT),
        name="merge",
    )(y_a, proj_a, y_s, proj_b, proj_b, proj_b, x2, w_glu, b_glu, w_out, norm_post)


def _prep_w_in(w_in):
    w_in = w_in.astype(BF16)
    o_bd = 4 * DN_WIDTH
    o_u = o_bd + 2 * DN_HEADS
    w_a = jnp.concatenate([w_in[..., :o_bd], w_in[..., o_u:o_u + D_MODEL]], axis=-1)
    pad = jnp.zeros(w_in.shape[:-1] + (LANES - 2 * DN_HEADS,), w_in.dtype)
    w_b = jnp.concatenate([w_in[..., o_u + D_MODEL:], w_in[..., o_bd:o_u], pad], axis=-1)
    return w_a, w_b


def _pad_rows(a, rows=SUBLANES):
    return jnp.pad(a, ((0, 0), (0, rows - a.shape[1]), (0, 0)))


def _trunk(x, norm_pre, w_in, conv_w, a_log, dt_bias, head_norm, ssm_a_re, ssm_a_im, ssm_log_dt,
           ssm_b_re, ssm_b_im, ssm_c_re, ssm_c_im, ssm_d, w_glu, b_glu, w_out, norm_post,
           *, tm=256, ts_dn=256, bt_dn=128, dn_heads=8, ts_s5=2048):
    bsz, seq, _ = x.shape
    depth = w_in.shape[0]
    ts_s5 = min(ts_s5, seq)
    x2 = x.reshape(bsz * seq, D_MODEL)
    w_a, w_b = _prep_w_in(w_in)
    w_glu_b = w_glu.astype(BF16)
    w_out_b = w_out.astype(BF16)
    s5_tw, s5_v, s5_sc = _s5_weights(*_s5_params(ssm_a_re, ssm_a_im, ssm_log_dt, ssm_b_re, ssm_b_im),
                                     ssm_c_re, ssm_c_im)
    at_decay = lambda p: jnp.pad(p, ((0, 0), (DECAY_LANE0, LANES - DECAY_LANE0 - DN_HEADS)))
    gate_p = _pad_rows(jnp.stack([at_decay(a_log), at_decay(dt_bias)], axis=1))
    conv_w8 = _pad_rows(conv_w)
    head_n = _pad_rows(head_norm[:, None, :])
    row = lambda p: p[:, None, :]
    for li in range(depth):
        proj_a, proj_b = _in_proj(x2, row(norm_pre), w_a, w_b, conv_w8, gate_p, li, seq, tm)
        y_a = _deltanet(proj_a, proj_b, head_n, li, bsz, seq, ts_dn, bt_dn, dn_heads)
        y_s = _s5(proj_a, s5_tw, s5_v, s5_sc, row(ssm_d), li, bsz, seq, ts_s5)
        x2 = _merge(y_a, y_s, proj_a, proj_b, x2, w_glu_b, row(b_glu), w_out_b, row(norm_post), li, tm)
    return x2.reshape(bsz, seq, D_MODEL)


def kernel(x, norm_pre, w_in, conv_w, a_log, dt_bias, head_norm, ssm_a_re, ssm_a_im, ssm_log_dt,
           ssm_b_re, ssm_b_im, ssm_c_re, ssm_c_im, ssm_d, w_glu, b_glu, w_out, norm_post):
    return _trunk(x, norm_pre, w_in, conv_w, a_log, dt_bias, head_norm, ssm_a_re, ssm_a_im, ssm_log_dt,
                  ssm_b_re, ssm_b_im, ssm_c_re, ssm_c_im, ssm_d, w_glu, b_glu, w_out, norm_post)
```

```python
import functools

import jax
import jax.numpy as jnp
from jax import lax
from jax.experimental import pallas as pl
from jax.experimental.pallas import tpu as pltpu

D_MODEL = 1024
DN_HEADS = 8
DN_HEAD_DIM = 128
DN_WIDTH = DN_HEADS * DN_HEAD_DIM
CONV_K = 4
CHUNK = 64
SSM_GROUP = 16
SSM_GROUPS = 64
SSM_STATE = 64
EPS = 1e-6

LANES = 128
SUBLANES = 8
VMEM_LIMIT = 56 * 1024 * 1024

PROJ_A_W = 5 * D_MODEL
PROJ_B_W = 3 * D_MODEL + LANES
A_ZA, A_U = 3, 4
B_ZB, B_RA, B_RB = 0, 1, 2
PROJ_B_BD = 3 * D_MODEL // LANES
DECAY_LANE0 = DN_HEADS
GATE_BLOCK = 256

S5_L = 16
S5_LC = S5_L * SSM_GROUP
S5_GB = LANES // SSM_GROUP
S5_NPOW = 8

F32 = jnp.float32
BF16 = jnp.bfloat16
HIGHEST = lax.Precision.HIGHEST


def _sigmoid(x):
    return 0.5 * jnp.tanh(0.5 * x) + 0.5


def _silu(x):
    hx = 0.5 * x
    return hx * jnp.tanh(hx) + hx


def _gelu_tanh(x):
    c = 0.7978845608028654
    return 0.5 * x * (1.0 + jnp.tanh(c * (x + 0.044715 * (x * x * x))))


def _softplus(x):
    return jnp.maximum(x, 0.0) + jnp.log(1.0 + jnp.exp(-jnp.abs(x)))


def _mm(a, b):
    return jnp.dot(a.astype(BF16), b.astype(BF16), preferred_element_type=F32)


def _mm_nt(a, b):
    return lax.dot_general(a.astype(BF16), b.astype(BF16), (((1,), (1,)), ((), ())),
                           preferred_element_type=F32)


def _mm_tn(a, b):
    return lax.dot_general(a.astype(BF16), b.astype(BF16), (((0,), (0,)), ((), ())),
                           preferred_element_type=F32)


def _mm_f32(a, b):
    return jnp.dot(a, b, precision=HIGHEST, preferred_element_type=F32)


def _mm_nt_f32(a, b):
    return lax.dot_general(a, b, (((1,), (1,)), ((), ())), precision=HIGHEST,
                           preferred_element_type=F32)


def _in_proj_kernel(x_ref, g_ref, wa_ref, wb_ref, cw_ref, gp_ref, pa_ref, pb_ref, halo, *, tm, tiles_per_seq):
    i = pl.program_id(0)

    @pl.when(i % tiles_per_seq == 0)
    def _():
        halo[...] = jnp.zeros_like(halo)

    x = x_ref[...]
    h = x * lax.rsqrt(jnp.mean(x * x, axis=-1, keepdims=True) + EPS) * g_ref[...]
    hb = h.astype(BF16)
    bd = jnp.dot(hb, wb_ref[:, PROJ_B_BD * LANES:], preferred_element_type=F32)
    g = -jnp.exp(gp_ref[0:1, :]) * _softplus(bd + gp_ref[1:2, :])
    sb = min(tm, GATE_BLOCK)
    row = lax.broadcasted_iota(jnp.int32, (sb, sb), 0)
    col = lax.broadcasted_iota(jnp.int32, (sb, sb), 1)
    shift = CHUNK.bit_length() - 1
    tril = (((row >> shift) == (col >> shift)) & (row >= col)).astype(F32)
    g_cum = jnp.concatenate([_mm_f32(tril, g[r0:r0 + sb, :]) for r0 in range(0, tm, sb)], axis=0)
    lane = lax.broadcasted_iota(jnp.int32, (tm, LANES), 1)
    pb_ref[:, PROJ_B_BD * LANES:] = jnp.where(lane < DECAY_LANE0, _sigmoid(bd), g_cum)

    wins = []
    for kind in range(3):
        a = kind * DN_WIDTH
        raw = jnp.dot(hb, wa_ref[:, a:a + DN_WIDTH], preferred_element_type=F32)
        wins.append(jnp.concatenate([halo[kind], raw], axis=0))
        halo[kind] = raw[tm - SUBLANES:, :]
    for a in range(3 * DN_WIDTH, PROJ_A_W, D_MODEL):
        pa_ref[:, a:a + D_MODEL] = jnp.dot(hb, wa_ref[:, a:a + D_MODEL], preferred_element_type=F32)
    for a in range(0, PROJ_B_BD * LANES, D_MODEL):
        pb_ref[:, a:a + D_MODEL] = jnp.dot(hb, wb_ref[:, a:a + D_MODEL], preferred_element_type=F32)

    scale = DN_HEAD_DIM ** -0.5
    for kind in range(3):
        a = kind * DN_WIDTH
        ws = wins[kind]
        w = cw_ref[:, a:a + DN_WIDTH]
        acc = ws[SUBLANES:] * w[CONV_K - 1:CONV_K]
        for j in range(CONV_K - 1):
            acc = acc + pltpu.roll(ws, CONV_K - 1 - j, axis=0)[SUBLANES:] * w[j:j + 1]
        y = _silu(acc)
        for hl in range(DN_HEADS):
            lo = hl * LANES
            yh = y[:, lo:lo + LANES]
            if kind == 0:
                yh = yh * (lax.rsqrt(jnp.sum(yh * yh, axis=-1, keepdims=True) + EPS) * scale)
            elif kind == 1:
                yh = yh * lax.rsqrt(jnp.sum(yh * yh, axis=-1, keepdims=True) + EPS)
            pa_ref[:, a + lo:a + lo + LANES] = yh


def _in_proj(x2, gain, w_a, w_b, conv_w8, gate_p, li, seq, tm):
    t = x2.shape[0]
    return pl.pallas_call(
        functools.partial(_in_proj_kernel, tm=tm, tiles_per_seq=seq // tm),
        grid=(t // tm,),
        in_specs=[
            pl.BlockSpec((tm, D_MODEL), lambda i: (i, 0)),
            pl.BlockSpec((None, 1, D_MODEL), lambda i: (li, 0, 0)),
            pl.BlockSpec((None, D_MODEL, PROJ_A_W), lambda i: (li, 0, 0), pipeline_mode=pl.Buffered(1)),
            pl.BlockSpec((None, D_MODEL, PROJ_B_W), lambda i: (li, 0, 0), pipeline_mode=pl.Buffered(1)),
            pl.BlockSpec((None, SUBLANES, 3 * DN_WIDTH), lambda i: (li, 0, 0)),
            pl.BlockSpec((None, SUBLANES, LANES), lambda i: (li, 0, 0)),
        ],
        out_specs=[
            pl.BlockSpec((tm, PROJ_A_W), lambda i: (i, 0)),
            pl.BlockSpec((tm, PROJ_B_W), lambda i: (i, 0)),
        ],
        out_shape=[
            jax.ShapeDtypeStruct((t, PROJ_A_W), F32),
            jax.ShapeDtypeStruct((t, PROJ_B_W), F32),
        ],
        scratch_shapes=[pltpu.VMEM((3, SUBLANES, DN_WIDTH), F32)],
        compiler_params=pltpu.CompilerParams(
            dimension_semantics=("arbitrary",), vmem_limit_bytes=VMEM_LIMIT),
        name="in_proj",
    )(x2, gain, w_a, w_b, conv_w8, gate_p)


def _neumann_inverse(l_mats, block):
    n = l_mats[0].shape[0]
    row = lax.broadcasted_iota(jnp.int32, (n, n), 0)
    col = lax.broadcasted_iota(jnp.int32, (n, n), 1)
    eye = jnp.where(row == col, 1.0, 0.0)
    ms = [-l for l in l_mats]
    ps = [eye + m for m in ms]
    for _ in range((block - 1).bit_length() - 1):
        mbs = [m.astype(BF16) for m in ms]
        ms = [jnp.dot(mb, mb, preferred_element_type=F32) for mb in mbs]
        ps = [p + _mm(p, m) for p, m in zip(ps, ms)]
    return ps


def _dn_kernel(q_ref, k_ref, v_ref, bd_ref, za_ref, ra_ref, hn_ref, o_ref, state, *, ts, bt, heads):
    hb = pl.program_id(1)
    t = pl.program_id(2)
    nc = bt // CHUNK

    @pl.when(t == 0)
    def _():
        state[...] = jnp.zeros_like(state)

    head_gain = hn_ref[0:1, :]
    lane = lax.broadcasted_iota(jnp.int32, (bt, LANES), 1)
    sub = lax.broadcasted_iota(jnp.int32, (LANES, bt), 0)
    row = lax.broadcasted_iota(jnp.int32, (bt, bt), 0)
    col = lax.broadcasted_iota(jnp.int32, (bt, bt), 1)
    shift = CHUNK.bit_length() - 1
    same = (row >> shift) == (col >> shift)
    causal = same & (row >= col)
    strict = same & (row > col)

    units = [(st, hl) for st in range(ts // bt) for hl in range(heads)]
    gates = []
    for st in range(ts // bt):
        bd = bd_ref[st * bt:(st + 1) * bt, :]
        gates.append((bd, jnp.transpose(bd)))

    qn, kn, vv, beta, g_col, decay = [], [], [], [], [], []
    for st, hl in units:
        tok = slice(st * bt, (st + 1) * bt)
        qn.append(q_ref[tok, hl * LANES:(hl + 1) * LANES])
        kn.append(k_ref[tok, hl * LANES:(hl + 1) * LANES])
        vv.append(v_ref[tok, hl * LANES:(hl + 1) * LANES])
        hg = hb * heads + hl
        bd, bd_t = gates[st]
        beta.append(jnp.sum(jnp.where(lane == hg, bd, 0.0), axis=-1, keepdims=True))
        gc = jnp.sum(jnp.where(lane == hg + DECAY_LANE0, bd, 0.0), axis=-1, keepdims=True)
        gr = jnp.sum(jnp.where(sub == hg + DECAY_LANE0, bd_t, 0.0), axis=0, keepdims=True)
        g_col.append(gc)
        decay.append(jnp.exp(jnp.where(causal, gc - gr, -jnp.inf)))

    kb = [k * b for k, b in zip(kn, beta)]
    knb = [k.astype(BF16) for k in kn]
    gram = [_mm_nt(jnp.concatenate([a, q], axis=0), b) for a, q, b in zip(kb, qn, knb)]
    l_mats = [jnp.where(strict, gm[:bt] * d, 0.0) for gm, d in zip(gram, decay)]
    a_qk = [gm[bt:] * d for gm, d in zip(gram, decay)]
    t_inv = _neumann_inverse(l_mats, CHUNK)
    e_g = [jnp.exp(g) for g in g_col]
    sol = [_mm(ti, jnp.concatenate([v * b, k * e], axis=-1))
           for ti, v, b, k, e in zip(t_inv, vv, beta, kb, e_g)]
    a_sol = [_mm(a, s) for a, s in zip(a_qk, sol)]
    q_eff = [q * e - a[:, DN_HEAD_DIM:] for q, e, a in zip(qn, e_g, a_sol)]
    g_last = [jnp.concatenate(
        [jnp.broadcast_to(g[(c + 1) * CHUNK - 1:(c + 1) * CHUNK, :], (CHUNK, 1)) for c in range(nc)], axis=0)
        for g in g_col]
    k_dec = [k * jnp.exp(gl - g) for k, gl, g in zip(kn, g_last, g_col)]
    gamma = [jnp.exp(gl) for gl in g_last]
    kt_sol = [[_mm_tn(kd[c * CHUNK:(c + 1) * CHUNK, :], s[c * CHUNK:(c + 1) * CHUNK, :]) for c in range(nc)]
              for kd, s in zip(k_dec, sol)]

    s_list = [state[hl] for hl in range(heads)]
    for st in range(ts // bt):
        for c in range(nc):
            rows = slice(c * CHUNK, (c + 1) * CHUNK)
            for hl in range(heads):
                i = st * heads + hl
                s = s_list[hl]
                on_s = _mm(jnp.concatenate([q_eff[i][rows, :], kt_sol[i][c][:, DN_HEAD_DIM:]], axis=0), s)
                o = on_s[:CHUNK] + a_sol[i][rows, :DN_HEAD_DIM]
                s_list[hl] = s * gamma[i][c * CHUNK:c * CHUNK + 1, :] - on_s[CHUNK:] + kt_sol[i][c][:, :DN_HEAD_DIM]
                o = o * lax.rsqrt(jnp.mean(o * o, axis=-1, keepdims=True) + EPS) * head_gain
                tok = slice(st * bt + c * CHUNK, st * bt + (c + 1) * CHUNK)
                cols = slice(hl * LANES, (hl + 1) * LANES)
                o_ref[tok, cols] = _sigmoid(ra_ref[tok, cols]) * (o * _silu(za_ref[tok, cols]))

    for hl in range(heads):
        state[hl] = s_list[hl]


def _deltanet(proj_a, proj_b, head_n, li, bsz, seq, ts, bt, heads):
    nt = seq // ts
    wid = heads * LANES
    nqb = DN_WIDTH // wid
    tok = lambda b, h, t: b * nt + t
    return pl.pallas_call(
        functools.partial(_dn_kernel, ts=ts, bt=bt, heads=heads),
        grid=(bsz, DN_HEADS // heads, nt),
        in_specs=[
            pl.BlockSpec((ts, wid), lambda b, h, t: (tok(b, h, t), h)),
            pl.BlockSpec((ts, wid), lambda b, h, t: (tok(b, h, t), nqb + h)),
            pl.BlockSpec((ts, wid), lambda b, h, t: (tok(b, h, t), 2 * nqb + h)),
            pl.BlockSpec((ts, LANES), lambda b, h, t: (tok(b, h, t), PROJ_B_BD)),
            pl.BlockSpec((ts, wid), lambda b, h, t: (tok(b, h, t), A_ZA * nqb + h)),
            pl.BlockSpec((ts, wid), lambda b, h, t: (tok(b, h, t), B_RA * nqb + h)),
            pl.BlockSpec((None, SUBLANES, LANES), lambda b, h, t: (li, 0, 0)),
        ],
        out_specs=pl.BlockSpec((ts, wid), lambda b, h, t: (tok(b, h, t), h)),
        out_shape=jax.ShapeDtypeStruct((bsz * seq, DN_WIDTH), F32),
        scratch_shapes=[pltpu.VMEM((heads, DN_HEAD_DIM, DN_HEAD_DIM), F32)],
        compiler_params=pltpu.CompilerParams(
            dimension_semantics=("arbitrary", "arbitrary", "arbitrary"), vmem_limit_bytes=VMEM_LIMIT),
        name="deltanet",
    )(proj_a, proj_a, proj_a, proj_b, proj_a, proj_b, head_n)


def _s5_param_kernel(are, aim, ldt, arer, aimr, ldtr, brt, bit, pwr, pwi, sqr, sqi, nsqi, bbr, bbi):
    ar = are[0]
    ai = aim[0]
    dt = jnp.exp(ldt[0])
    for k in range(S5_L + 1):
        mag = jnp.exp(ar * dt * float(k))
        ang = ai * dt * float(k)
        pwr[0, k] = mag * jnp.cos(ang)
        pwi[0, k] = mag * jnp.sin(ang)
    pr = pwr[0, S5_L]
    pi = pwi[0, S5_L]
    for j in range(S5_NPOW):
        sqr[0, j] = pr
        sqi[0, j] = pi
        nsqi[0, j] = -pi
        pr, pi = pr * pr - pi * pi, 2.0 * (pr * pi)
    ar = arer[0]
    ai = aimr[0]
    dt = jnp.exp(ldtr[0])
    mag = jnp.exp(ar * dt)
    lr = mag * jnp.cos(ai * dt)
    li = mag * jnp.sin(ai * dt)
    den = ar * ar + ai * ai
    fr = ((lr - 1.0) * ar + li * ai) / den
    fi = (li * ar - (lr - 1.0) * ai) / den
    br = brt[0]
    bi = bit[0]
    bbr[0] = fr * br - fi * bi
    bbi[0] = fr * bi + fi * br


def _s5_params(a_re, a_im, log_dt, b_re, b_im):
    nl = a_re.shape[0]
    g, n, c = SSM_GROUPS, SSM_STATE, SSM_GROUP
    ldt = jnp.broadcast_to(log_dt[..., None], (nl, g, n))
    rep = lambda a: jnp.repeat(a, c, axis=1)
    brt = jnp.swapaxes(b_re, 2, 3).reshape(nl, g * c, n)
    bit = jnp.swapaxes(b_im, 2, 3).reshape(nl, g * c, n)
    small = pl.BlockSpec((1, g, n), lambda l: (l, 0, 0))
    big = pl.BlockSpec((1, g * c, n), lambda l: (l, 0, 0))
    powr = pl.BlockSpec((1, S5_L + 1, g, n), lambda l: (l, 0, 0, 0))
    sqs = pl.BlockSpec((1, S5_NPOW, g, n), lambda l: (l, 0, 0, 0))
    pw_shape = jax.ShapeDtypeStruct((nl, S5_L + 1, g, n), F32)
    sq_shape = jax.ShapeDtypeStruct((nl, S5_NPOW, g, n), F32)
    bb_shape = jax.ShapeDtypeStruct((nl, g * c, n), F32)
    return pl.pallas_call(
        _s5_param_kernel,
        grid=(nl,),
        in_specs=[small, small, small, big, big, big, big, big],
        out_specs=[powr, powr, sqs, sqs, sqs, big, big],
        out_shape=[pw_shape, pw_shape, sq_shape, sq_shape, sq_shape, bb_shape, bb_shape],
        name="s5_params",
    )(a_re, a_im, ldt, rep(a_re), rep(a_im), rep(ldt), brt, bit)


def _s5_toeplitz_kernel(pwr, pwi, btr, bti, cr_ref, ci_ref, kt, vr, nvi, wr, wi):
    shape = (SSM_GROUP, SSM_STATE)
    for gl in range(S5_GB):
        pr = pwr[gl]
        pi = pwi[gl]
        rows = lambda p, ms: jnp.concatenate([jnp.broadcast_to(p[m:m + 1, :], shape) for m in ms], axis=0)
        tile = lambda a: jnp.concatenate([a] * S5_L, axis=0)
        cr, ci = tile(cr_ref[gl]), tile(ci_ref[gl])
        br, bi = tile(btr[gl]), tile(bti[gl])
        lr, li = rows(pr, range(S5_L)), rows(pi, range(S5_L))
        qr = cr * lr - ci * li
        qi = cr * li + ci * lr
        kt[gl] = _mm_nt_f32(qr, btr[gl]) - _mm_nt_f32(qi, bti[gl])
        lr, li = rows(pr, range(1, S5_L + 1)), rows(pi, range(1, S5_L + 1))
        vr[gl] = cr * lr - ci * li
        nvi[gl] = -(cr * li + ci * lr)
        lr, li = rows(pr, range(S5_L - 1, -1, -1)), rows(pi, range(S5_L - 1, -1, -1))
        wr[gl] = lr * br - li * bi
        wi[gl] = lr * bi + li * br


def _s5_weights(pwr, pwi, sqr, sqi, nsqi, bbr, bbi, c_re, c_im):
    nl = pwr.shape[0]
    g, n, c, ln = SSM_GROUPS, SSM_STATE, SSM_GROUP, S5_L
    by_group = lambda p: jnp.transpose(p, (0, 2, 1, 3))
    btr, bti = bbr.reshape(nl, g, c, n), bbi.reshape(nl, g, c, n)
    spec = lambda a, b: pl.BlockSpec((None, S5_GB, a, b), lambda l, q: (l, q, 0, 0))
    shp = lambda a, b: jax.ShapeDtypeStruct((nl, g, a, b), F32)
    kt, vr, nvi, wr, wi = pl.pallas_call(
        _s5_toeplitz_kernel,
        grid=(nl, g // S5_GB),
        in_specs=[spec(ln + 1, n)] * 2 + [spec(c, n)] * 4,
        out_specs=[spec(S5_LC, c)] + [spec(S5_LC, n)] * 4,
        out_shape=[shp(S5_LC, c)] + [shp(S5_LC, n)] * 4,
        name="s5_toeplitz",
    )(by_group(pwr), by_group(pwi), btr, bti, c_re, c_im)

    kall = jnp.transpose(kt.reshape(nl, g, ln, c, c), (0, 1, 4, 2, 3)).reshape(nl, g, c, S5_LC)
    rows = [jnp.pad(kall[..., :S5_LC - c * s], ((0, 0), (0, 0), (0, 0), (c * s, 0))) for s in range(ln)]
    toep = jnp.stack(rows, axis=2).reshape(nl, g, S5_LC, S5_LC)
    tw = jnp.concatenate([toep, wr, wi], axis=-1).astype(BF16)
    vt = jnp.concatenate([vr, nvi], axis=-1).astype(BF16)
    pair = lambda a, b: jnp.concatenate([by_group(a), by_group(b)], axis=-1)
    sc = jnp.concatenate([pair(sqr, sqr), pair(nsqi, sqi)], axis=-1)
    return tw, vt, sc


def _granule_transpose(xs):
    xs = list(xs)
    lane = lax.broadcasted_iota(jnp.int32, xs[0].shape, 1)
    gshift = SSM_GROUP.bit_length() - 1
    for m in range(S5_GB.bit_length() - 1):
        d = SSM_GROUP << m
        upper = ((lane >> (gshift + m)) & 1) == 1
        nxt = list(xs)
        for ia in range(S5_GB):
            if (ia >> m) & 1:
                continue
            ib = ia | (1 << m)
            a, b = xs[ia], xs[ib]
            nxt[ia] = jnp.where(upper, pltpu.roll(b, d, axis=1), a)
            nxt[ib] = jnp.where(upper, b, pltpu.roll(a, LANES - d, axis=1))
        xs = nxt
    return xs


def _s5_kernel(u_ref, tw_ref, v_ref, sc_ref, d_ref, y_ref, carry, *, ts):
    t = pl.program_id(1)
    i = pl.program_id(2)
    r = ts // S5_L
    half = SSM_STATE

    @pl.when(t == 0)
    def _():
        carry[pl.ds(i * S5_GB, S5_GB)] = jnp.zeros((S5_GB, SUBLANES, LANES), F32)

    a_rows = [u_ref[pl.ds(s, r, stride=S5_L), :] for s in range(S5_L)]
    folded = [_granule_transpose(a_rows[h * S5_GB:(h + 1) * S5_GB]) for h in range(S5_L // S5_GB)]
    row = lax.broadcasted_iota(jnp.int32, (r, LANES), 0)
    row8 = lax.broadcasted_iota(jnp.int32, (SUBLANES, LANES), 0)
    groups = range(S5_GB)
    txs = [jnp.dot(jnp.concatenate([f[gl] for f in folded], axis=1).astype(BF16), tw_ref[gl],
                   preferred_element_type=F32) for gl in groups]
    scs = [sc_ref[gl] for gl in groups]
    c8s = [carry[i * S5_GB + gl] for gl in groups]
    hs_ = []
    for tx, sc, c8 in zip(txs, scs, c8s):
        x = tx[:, S5_LC:]
        inj = sc[0:1, :LANES] * c8 + sc[0:1, LANES:] * pltpu.roll(c8, half, axis=1)
        hs_.append(jnp.concatenate([x[:SUBLANES] + jnp.where(row8 == 0, inj, 0.0), x[SUBLANES:]], axis=0))
    for j in range(r.bit_length() - 1):
        sh = 1 << j
        shifted = [jnp.where(row >= sh, pltpu.roll(h, sh, axis=0), 0.0) for h in hs_]
        hs_ = [h + sc[j:j + 1, :LANES] * s + sc[j:j + 1, LANES:] * pltpu.roll(s, half, axis=1)
               for h, s, sc in zip(hs_, shifted, scs)]
    y_groups = []
    for gl, (tx, h, c8) in enumerate(zip(txs, hs_, c8s)):
        h_prev = jnp.where(row >= 1, pltpu.roll(h, 1, axis=0), jnp.broadcast_to(c8[0:1], (r, LANES)))
        y_groups.append(tx[:, :S5_LC] + _mm_nt(h_prev, v_ref[gl]))
        carry[i * S5_GB + gl] = jnp.broadcast_to(h[r - 1:r], (SUBLANES, LANES))
    d_skip = d_ref[...]
    for h in range(S5_L // S5_GB):
        unfolded = _granule_transpose([y[:, h * LANES:(h + 1) * LANES] for y in y_groups])
        for q in range(S5_GB):
            s = h * S5_GB + q
            y_ref[pl.ds(s, r, stride=S5_L), :] = unfolded[q] + d_skip * a_rows[s]


def _s5(proj, tw, v, sc, d_skip, li, bsz, seq, ts):
    nt = seq // ts
    ncb = D_MODEL // LANES
    grp = lambda a, b: pl.BlockSpec((None, S5_GB, a, b), lambda b_, t, i: (li, i, 0, 0))
    return pl.pallas_call(
        functools.partial(_s5_kernel, ts=ts),
        grid=(bsz, nt, ncb),
        in_specs=[
            pl.BlockSpec((ts, LANES), lambda b, t, i: (b * nt + t, A_U * ncb + i)),
            grp(S5_LC, S5_LC + 2 * SSM_STATE),
            grp(S5_LC, 2 * SSM_STATE),
            grp(S5_NPOW, 2 * LANES),
            pl.BlockSpec((None, 1, LANES), lambda b, t, i: (li, 0, i)),
        ],
        out_specs=pl.BlockSpec((ts, LANES), lambda b, t, i: (b * nt + t, i)),
        out_shape=jax.ShapeDtypeStruct((bsz * seq, D_MODEL), F32),
        scratch_shapes=[pltpu.VMEM((SSM_GROUPS, SUBLANES, LANES), F32)],
        compiler_params=pltpu.CompilerParams(
            dimension_semantics=("arbitrary", "arbitrary", "arbitrary"), vmem_limit_bytes=VMEM_LIMIT),
        name="s5",
    )(proj, tw, v, sc, d_skip)


def _merge_kernel(ya_ref, ys_ref, zb_ref, rb_ref, x_ref, wglu_ref, bglu_ref, wout_ref, np_ref, o_ref):
    y = _gelu_tanh(ys_ref[...])
    y = y * _sigmoid(jnp.dot(y.astype(BF16), wglu_ref[...], preferred_element_type=F32) + bglu_ref[...])
    y_b = y * _silu(zb_ref[...])
    merged = ya_ref[...] + _sigmoid(rb_ref[...]) * y_b
    out = jnp.dot(merged.astype(BF16), wout_ref[...], preferred_element_type=F32)
    out = out * lax.rsqrt(jnp.mean(out * out, axis=-1, keepdims=True) + EPS) * np_ref[...]
    o_ref[...] = x_ref[...] + out


def _merge(y_a, y_s, proj_b, x2, w_glu, b_glu, w_out, norm_post, li, tm):
    t = x2.shape[0]
    col = lambda c: pl.BlockSpec((tm, D_MODEL), lambda i: (i, c))
    vec = pl.BlockSpec((None, 1, D_MODEL), lambda i: (li, 0, 0))
    mat = pl.BlockSpec((None, D_MODEL, D_MODEL), lambda i: (li, 0, 0))
    return pl.pallas_call(
        _merge_kernel,
        grid=(t // tm,),
        in_specs=[col(0), col(0), col(B_ZB), col(B_RB), col(0), mat, vec, mat, vec],
        out_specs=col(0),
        out_shape=jax.ShapeDtypeStruct((t, D_MODEL), F32),
        compiler_params=pltpu.CompilerParams(
            dimension_semantics=("arbitrary",), vmem_limit_bytes=VMEM_LIMIT),
        name="merge",
    )(y_a, y_s, proj_b, proj_b, x2, w_glu, b_glu, w_out, norm_post)


def _prep_w_in(w_in):
    w_in = w_in.astype(BF16)
    o_bd = 4 * DN_WIDTH
    o_u = o_bd + 2 * DN_HEADS
    w_a = jnp.concatenate([w_in[..., :o_bd], w_in[..., o_u:o_u + D_MODEL]], axis=-1)
    pad = jnp.zeros(w_in.shape[:-1] + (LANES - 2 * DN_HEADS,), w_in.dtype)
    w_b = jnp.concatenate([w_in[..., o_u + D_MODEL:], w_in[..., o_bd:o_u], pad], axis=-1)
    return w_a, w_b


def _pad_rows(a, rows=SUBLANES):
    return jnp.pad(a, ((0, 0), (0, rows - a.shape[1]), (0, 0)))


def _trunk(x, norm_pre, w_in, conv_w, a_log, dt_bias, head_norm, ssm_a_re, ssm_a_im, ssm_log_dt,
           ssm_b_re, ssm_b_im, ssm_c_re, ssm_c_im, ssm_d, w_glu, b_glu, w_out, norm_post,
           *, tm=256, ts_dn=512, bt_dn=128, dn_heads=8, ts_s5=4096):
    bsz, seq, _ = x.shape
    depth = w_in.shape[0]
    ts_s5 = min(ts_s5, seq)
    x2 = x.reshape(bsz * seq, D_MODEL)
    w_a, w_b = _prep_w_in(w_in)
    w_glu_b = w_glu.astype(BF16)
    w_out_b = w_out.astype(BF16)
    s5_tw, s5_v, s5_sc = _s5_weights(*_s5_params(ssm_a_re, ssm_a_im, ssm_log_dt, ssm_b_re, ssm_b_im),
                                     ssm_c_re, ssm_c_im)
    at_decay = lambda p: jnp.pad(p, ((0, 0), (DECAY_LANE0, LANES - DECAY_LANE0 - DN_HEADS)))
    gate_p = _pad_rows(jnp.stack([at_decay(a_log), at_decay(dt_bias)], axis=1))
    conv_w8 = _pad_rows(conv_w)
    head_n = _pad_rows(head_norm[:, None, :])
    row = lambda p: p[:, None, :]
    for li in range(depth):
        proj_a, proj_b = _in_proj(x2, row(norm_pre), w_a, w_b, conv_w8, gate_p, li, seq, tm)
        y_a = _deltanet(proj_a, proj_b, head_n, li, bsz, seq, ts_dn, bt_dn, dn_heads)
        y_s = _s5(proj_a, s5_tw, s5_v, s5_sc, row(ssm_d), li, bsz, seq, ts_s5)
        x2 = _merge(y_a, y_s, proj_b, x2, w_glu_b, row(b_glu), w_out_b, row(norm_post), li, tm)
    return x2.reshape(bsz, seq, D_MODEL)


def kernel(x, norm_pre, w_in, conv_w, a_log, dt_bias, head_norm, ssm_a_re, ssm_a_im, ssm_log_dt,
           ssm_b_re, ssm_b_im, ssm_c_re, ssm_c_im, ssm_d, w_glu, b_glu, w_out, norm_post):
    return _trunk(x, norm_pre, w_in, conv_w, a_log, dt_bias, head_norm, ssm_a_re, ssm_a_im, ssm_log_dt,
                  ssm_b_re, ssm_b_im, ssm_c_re, ssm_c_im, ssm_d, w_glu, b_glu, w_out, norm_post)
```

```python
import functools

import jax
import jax.numpy as jnp
from jax import lax
from jax.experimental import pallas as pl
from jax.experimental.pallas import tpu as pltpu

D_MODEL = 1024
DN_HEADS = 8
DN_HEAD_DIM = 128
DN_WIDTH = DN_HEADS * DN_HEAD_DIM
CONV_K = 4
CHUNK = 64
SSM_GROUP = 16
SSM_GROUPS = 64
SSM_STATE = 64
EPS = 1e-6

LANES = 128
SUBLANES = 8
VMEM_LIMIT = 56 * 1024 * 1024

PROJ_A_W = 5 * D_MODEL
PROJ_B_W = 3 * D_MODEL + LANES
A_ZA, A_U = 3, 4
B_ZB, B_RA, B_RB = 0, 1, 2
PROJ_B_BD = 3 * D_MODEL // LANES
DECAY_LANE0 = DN_HEADS
GATE_BLOCK = 256

S5_L = 16
S5_LC = S5_L * SSM_GROUP
S5_GB = LANES // SSM_GROUP
S5_NPOW = 8

F32 = jnp.float32
BF16 = jnp.bfloat16
HIGHEST = lax.Precision.HIGHEST


def _sigmoid(x):
    return 0.5 * jnp.tanh(0.5 * x) + 0.5


def _silu(x):
    hx = 0.5 * x
    return hx * jnp.tanh(hx) + hx


def _gelu_tanh(x):
    c = 0.7978845608028654
    return 0.5 * x * (1.0 + jnp.tanh(c * (x + 0.044715 * (x * x * x))))


def _softplus(x):
    return jnp.maximum(x, 0.0) + jnp.log(1.0 + jnp.exp(-jnp.abs(x)))


def _mm(a, b):
    return jnp.dot(a.astype(BF16), b.astype(BF16), preferred_element_type=F32)


def _mm_nt(a, b):
    return lax.dot_general(a.astype(BF16), b.astype(BF16), (((1,), (1,)), ((), ())),
                           preferred_element_type=F32)


def _mm_tn(a, b):
    return lax.dot_general(a.astype(BF16), b.astype(BF16), (((0,), (0,)), ((), ())),
                           preferred_element_type=F32)


def _mm_f32(a, b):
    return jnp.dot(a, b, precision=HIGHEST, preferred_element_type=F32)


def _mm_nt_f32(a, b):
    return lax.dot_general(a, b, (((1,), (1,)), ((), ())), precision=HIGHEST,
                           preferred_element_type=F32)


def _in_proj_kernel(x_ref, g_ref, wa_ref, wb_ref, cw_ref, gp_ref, pa_ref, pb_ref, halo, *, tm, tiles_per_seq):
    i = pl.program_id(0)

    @pl.when(i % tiles_per_seq == 0)
    def _():
        halo[...] = jnp.zeros_like(halo)

    x = x_ref[...]
    h = x * lax.rsqrt(jnp.mean(x * x, axis=-1, keepdims=True) + EPS) * g_ref[...]
    hb = h.astype(BF16)
    bd = jnp.dot(hb, wb_ref[:, PROJ_B_BD * LANES:], preferred_element_type=F32)
    g = -jnp.exp(gp_ref[0:1, :]) * _softplus(bd + gp_ref[1:2, :])
    sb = min(tm, GATE_BLOCK)
    row = lax.broadcasted_iota(jnp.int32, (sb, sb), 0)
    col = lax.broadcasted_iota(jnp.int32, (sb, sb), 1)
    shift = CHUNK.bit_length() - 1
    tril = (((row >> shift) == (col >> shift)) & (row >= col)).astype(F32)
    g_cum = jnp.concatenate([_mm_f32(tril, g[r0:r0 + sb, :]) for r0 in range(0, tm, sb)], axis=0)
    lane = lax.broadcasted_iota(jnp.int32, (tm, LANES), 1)
    pb_ref[:, PROJ_B_BD * LANES:] = jnp.where(lane < DECAY_LANE0, _sigmoid(bd), g_cum)

    wins = []
    for kind in range(3):
        a = kind * DN_WIDTH
        raw = jnp.dot(hb, wa_ref[:, a:a + DN_WIDTH], preferred_element_type=F32)
        wins.append(jnp.concatenate([halo[kind], raw], axis=0))
        halo[kind] = raw[tm - SUBLANES:, :]
    for a in range(3 * DN_WIDTH, PROJ_A_W, D_MODEL):
        pa_ref[:, a:a + D_MODEL] = jnp.dot(hb, wa_ref[:, a:a + D_MODEL], preferred_element_type=F32)
    for a in range(0, PROJ_B_BD * LANES, D_MODEL):
        pb_ref[:, a:a + D_MODEL] = jnp.dot(hb, wb_ref[:, a:a + D_MODEL], preferred_element_type=F32)

    scale = DN_HEAD_DIM ** -0.5
    for kind in range(3):
        a = kind * DN_WIDTH
        ws = wins[kind]
        w = cw_ref[:, a:a + DN_WIDTH]
        acc = ws[SUBLANES:] * w[CONV_K - 1:CONV_K]
        for j in range(CONV_K - 1):
            acc = acc + pltpu.roll(ws, CONV_K - 1 - j, axis=0)[SUBLANES:] * w[j:j + 1]
        y = _silu(acc)
        for hl in range(DN_HEADS):
            lo = hl * LANES
            yh = y[:, lo:lo + LANES]
            if kind == 0:
                yh = yh * (lax.rsqrt(jnp.sum(yh * yh, axis=-1, keepdims=True) + EPS) * scale)
            elif kind == 1:
                yh = yh * lax.rsqrt(jnp.sum(yh * yh, axis=-1, keepdims=True) + EPS)
            pa_ref[:, a + lo:a + lo + LANES] = yh


def _in_proj(x2, gain, w_a, w_b, conv_w8, gate_p, li, seq, tm):
    t = x2.shape[0]
    return pl.pallas_call(
        functools.partial(_in_proj_kernel, tm=tm, tiles_per_seq=seq // tm),
        grid=(t // tm,),
        in_specs=[
            pl.BlockSpec((tm, D_MODEL), lambda i: (i, 0)),
            pl.BlockSpec((None, 1, D_MODEL), lambda i: (li, 0, 0)),
            pl.BlockSpec((None, D_MODEL, PROJ_A_W), lambda i: (li, 0, 0), pipeline_mode=pl.Buffered(1)),
            pl.BlockSpec((None, D_MODEL, PROJ_B_W), lambda i: (li, 0, 0), pipeline_mode=pl.Buffered(1)),
            pl.BlockSpec((None, SUBLANES, 3 * DN_WIDTH), lambda i: (li, 0, 0)),
            pl.BlockSpec((None, SUBLANES, LANES), lambda i: (li, 0, 0)),
        ],
        out_specs=[
            pl.BlockSpec((tm, PROJ_A_W), lambda i: (i, 0)),
            pl.BlockSpec((tm, PROJ_B_W), lambda i: (i, 0)),
        ],
        out_shape=[
            jax.ShapeDtypeStruct((t, PROJ_A_W), F32),
            jax.ShapeDtypeStruct((t, PROJ_B_W), F32),
        ],
        scratch_shapes=[pltpu.VMEM((3, SUBLANES, DN_WIDTH), F32)],
        compiler_params=pltpu.CompilerParams(
            dimension_semantics=("arbitrary",), vmem_limit_bytes=VMEM_LIMIT),
        name="in_proj",
    )(x2, gain, w_a, w_b, conv_w8, gate_p)


def _neumann_inverse(l_mats, block):
    n = l_mats[0].shape[0]
    row = lax.broadcasted_iota(jnp.int32, (n, n), 0)
    col = lax.broadcasted_iota(jnp.int32, (n, n), 1)
    eye = jnp.where(row == col, 1.0, 0.0)
    ms = [-l for l in l_mats]
    ps = [eye + m for m in ms]
    for _ in range((block - 1).bit_length() - 1):
        mbs = [m.astype(BF16) for m in ms]
        ms = [jnp.dot(mb, mb, preferred_element_type=F32) for mb in mbs]
        ps = [p + _mm(p, m) for p, m in zip(ps, ms)]
    return ps


def _dn_kernel(q_ref, k_ref, v_ref, bd_ref, za_ref, ra_ref, hn_ref, o_ref, state, *, ts, bt, heads):
    hb = pl.program_id(1)
    t = pl.program_id(2)
    nc = bt // CHUNK

    @pl.when(t == 0)
    def _():
        state[...] = jnp.zeros_like(state)

    head_gain = hn_ref[0:1, :]
    lane = lax.broadcasted_iota(jnp.int32, (bt, LANES), 1)
    sub = lax.broadcasted_iota(jnp.int32, (LANES, bt), 0)
    row = lax.broadcasted_iota(jnp.int32, (bt, bt), 0)
    col = lax.broadcasted_iota(jnp.int32, (bt, bt), 1)
    shift = CHUNK.bit_length() - 1
    same = (row >> shift) == (col >> shift)
    causal = same & (row >= col)
    strict = same & (row > col)

    units = [(st, hl) for st in range(ts // bt) for hl in range(heads)]
    gates = []
    for st in range(ts // bt):
        bd = bd_ref[st * bt:(st + 1) * bt, :]
        gates.append((bd, jnp.transpose(bd)))

    qn, kn, vv, beta, g_col, decay = [], [], [], [], [], []
    for st, hl in units:
        tok = slice(st * bt, (st + 1) * bt)
        qn.append(q_ref[tok, hl * LANES:(hl + 1) * LANES])
        kn.append(k_ref[tok, hl * LANES:(hl + 1) * LANES])
        vv.append(v_ref[tok, hl * LANES:(hl + 1) * LANES])
        hg = hb * heads + hl
        bd, bd_t = gates[st]
        beta.append(jnp.sum(jnp.where(lane == hg, bd, 0.0), axis=-1, keepdims=True))
        gc = jnp.sum(jnp.where(lane == hg + DECAY_LANE0, bd, 0.0), axis=-1, keepdims=True)
        gr = jnp.sum(jnp.where(sub == hg + DECAY_LANE0, bd_t, 0.0), axis=0, keepdims=True)
        g_col.append(gc)
        decay.append(jnp.exp(jnp.where(causal, gc - gr, -jnp.inf)))

    kb = [k * b for k, b in zip(kn, beta)]
    knb = [k.astype(BF16) for k in kn]
    gram = [_mm_nt(jnp.concatenate([a, q], axis=0), b) for a, q, b in zip(kb, qn, knb)]
    l_mats = [jnp.where(strict, gm[:bt] * d, 0.0) for gm, d in zip(gram, decay)]
    a_qk = [gm[bt:] * d for gm, d in zip(gram, decay)]
    t_inv = _neumann_inverse(l_mats, CHUNK)
    e_g = [jnp.exp(g) for g in g_col]
    sol = [_mm(ti, jnp.concatenate([v * b, k * e], axis=-1))
           for ti, v, b, k, e in zip(t_inv, vv, beta, kb, e_g)]
    a_sol = [_mm(a, s) for a, s in zip(a_qk, sol)]
    q_eff = [q * e - a[:, DN_HEAD_DIM:] for q, e, a in zip(qn, e_g, a_sol)]
    g_last = [jnp.concatenate(
        [jnp.broadcast_to(g[(c + 1) * CHUNK - 1:(c + 1) * CHUNK, :], (CHUNK, 1)) for c in range(nc)], axis=0)
        for g in g_col]
    k_dec = [k * jnp.exp(gl - g) for k, gl, g in zip(kn, g_last, g_col)]
    gamma = [jnp.exp(gl) for gl in g_last]
    kt_sol = [[_mm_tn(kd[c * CHUNK:(c + 1) * CHUNK, :], s[c * CHUNK:(c + 1) * CHUNK, :]) for c in range(nc)]
              for kd, s in zip(k_dec, sol)]

    s_list = [state[hl] for hl in range(heads)]
    for st in range(ts // bt):
        for c in range(nc):
            rows = slice(c * CHUNK, (c + 1) * CHUNK)
            for hl in range(heads):
                i = st * heads + hl
                s = s_list[hl]
                on_s = _mm(jnp.concatenate([q_eff[i][rows, :], kt_sol[i][c][:, DN_HEAD_DIM:]], axis=0), s)
                o = on_s[:CHUNK] + a_sol[i][rows, :DN_HEAD_DIM]
                s_list[hl] = s * gamma[i][c * CHUNK:c * CHUNK + 1, :] - on_s[CHUNK:] + kt_sol[i][c][:, :DN_HEAD_DIM]
                o = o * lax.rsqrt(jnp.mean(o * o, axis=-1, keepdims=True) + EPS) * head_gain
                tok = slice(st * bt + c * CHUNK, st * bt + (c + 1) * CHUNK)
                cols = slice(hl * LANES, (hl + 1) * LANES)
                o_ref[tok, cols] = _sigmoid(ra_ref[tok, cols]) * (o * _silu(za_ref[tok, cols]))

    for hl in range(heads):
        state[hl] = s_list[hl]


def _deltanet(proj_a, proj_b, head_n, li, bsz, seq, ts, bt, heads):
    nt = seq // ts
    wid = heads * LANES
    nqb = DN_WIDTH // wid
    tok = lambda b, h, t: b * nt + t
    return pl.pallas_call(
        functools.partial(_dn_kernel, ts=ts, bt=bt, heads=heads),
        grid=(bsz, DN_HEADS // heads, nt),
        in_specs=[
            pl.BlockSpec((ts, wid), lambda b, h, t: (tok(b, h, t), h)),
            pl.BlockSpec((ts, wid), lambda b, h, t: (tok(b, h, t), nqb + h)),
            pl.BlockSpec((ts, wid), lambda b, h, t: (tok(b, h, t), 2 * nqb + h)),
            pl.BlockSpec((ts, LANES), lambda b, h, t: (tok(b, h, t), PROJ_B_BD)),
            pl.BlockSpec((ts, wid), lambda b, h, t: (tok(b, h, t), A_ZA * nqb + h)),
            pl.BlockSpec((ts, wid), lambda b, h, t: (tok(b, h, t), B_RA * nqb + h)),
            pl.BlockSpec((None, SUBLANES, LANES), lambda b, h, t: (li, 0, 0)),
        ],
        out_specs=pl.BlockSpec((ts, wid), lambda b, h, t: (tok(b, h, t), h)),
        out_shape=jax.ShapeDtypeStruct((bsz * seq, DN_WIDTH), F32),
        scratch_shapes=[pltpu.VMEM((heads, DN_HEAD_DIM, DN_HEAD_DIM), F32)],
        compiler_params=pltpu.CompilerParams(
            dimension_semantics=("arbitrary", "arbitrary", "arbitrary"), vmem_limit_bytes=VMEM_LIMIT),
        name="deltanet",
    )(proj_a, proj_a, proj_a, proj_b, proj_a, proj_b, head_n)


def _s5_param_kernel(are, aim, ldt, arer, aimr, ldtr, brt, bit, pwr, pwi, sqr, sqi, nsqi, bbr, bbi):
    ar = are[0]
    ai = aim[0]
    dt = jnp.exp(ldt[0])
    for k in range(S5_L + 1):
        mag = jnp.exp(ar * dt * float(k))
        ang = ai * dt * float(k)
        pwr[0, k] = mag * jnp.cos(ang)
        pwi[0, k] = mag * jnp.sin(ang)
    pr = pwr[0, S5_L]
    pi = pwi[0, S5_L]
    for j in range(S5_NPOW):
        sqr[0, j] = pr
        sqi[0, j] = pi
        nsqi[0, j] = -pi
        pr, pi = pr * pr - pi * pi, 2.0 * (pr * pi)
    ar = arer[0]
    ai = aimr[0]
    dt = jnp.exp(ldtr[0])
    mag = jnp.exp(ar * dt)
    lr = mag * jnp.cos(ai * dt)
    li = mag * jnp.sin(ai * dt)
    den = ar * ar + ai * ai
    fr = ((lr - 1.0) * ar + li * ai) / den
    fi = (li * ar - (lr - 1.0) * ai) / den
    br = brt[0]
    bi = bit[0]
    bbr[0] = fr * br - fi * bi
    bbi[0] = fr * bi + fi * br


def _s5_params(a_re, a_im, log_dt, b_re, b_im):
    nl = a_re.shape[0]
    g, n, c = SSM_GROUPS, SSM_STATE, SSM_GROUP
    ldt = jnp.broadcast_to(log_dt[..., None], (nl, g, n))
    rep = lambda a: jnp.repeat(a, c, axis=1)
    brt = jnp.swapaxes(b_re, 2, 3).reshape(nl, g * c, n)
    bit = jnp.swapaxes(b_im, 2, 3).reshape(nl, g * c, n)
    small = pl.BlockSpec((1, g, n), lambda l: (l, 0, 0))
    big = pl.BlockSpec((1, g * c, n), lambda l: (l, 0, 0))
    powr = pl.BlockSpec((1, S5_L + 1, g, n), lambda l: (l, 0, 0, 0))
    sqs = pl.BlockSpec((1, S5_NPOW, g, n), lambda l: (l, 0, 0, 0))
    pw_shape = jax.ShapeDtypeStruct((nl, S5_L + 1, g, n), F32)
    sq_shape = jax.ShapeDtypeStruct((nl, S5_NPOW, g, n), F32)
    bb_shape = jax.ShapeDtypeStruct((nl, g * c, n), F32)
    return pl.pallas_call(
        _s5_param_kernel,
        grid=(nl,),
        in_specs=[small, small, small, big, big, big, big, big],
        out_specs=[powr, powr, sqs, sqs, sqs, big, big],
        out_shape=[pw_shape, pw_shape, sq_shape, sq_shape, sq_shape, bb_shape, bb_shape],
        name="s5_params",
    )(a_re, a_im, ldt, rep(a_re), rep(a_im), rep(ldt), brt, bit)


def _s5_toeplitz_kernel(pwr, pwi, btr, bti, cr_ref, ci_ref, tw, vt):
    shape = (SSM_GROUP, SSM_STATE)
    lane = lax.broadcasted_iota(jnp.int32, (SSM_GROUP, S5_LC), 1)
    for gl in range(S5_GB):
        pr = pwr[gl]
        pi = pwi[gl]
        rows = lambda p, ms: jnp.concatenate([jnp.broadcast_to(p[m:m + 1, :], shape) for m in ms], axis=0)
        tile = lambda a: jnp.concatenate([a] * S5_L, axis=0)
        cr, ci = tile(cr_ref[gl]), tile(ci_ref[gl])
        br, bi = tile(btr[gl]), tile(bti[gl])
        lr, li = rows(pr, range(S5_L)), rows(pi, range(S5_L))
        qr = cr * lr - ci * li
        qi = cr * li + ci * lr
        kall = _mm_nt_f32(btr[gl], qr) - _mm_nt_f32(bti[gl], qi)
        for s in range(S5_L):
            off = s * SSM_GROUP
            blk = kall if s == 0 else jnp.where(lane >= off, pltpu.roll(kall, off, axis=1), 0.0)
            tw[gl, off:off + SSM_GROUP, 0:S5_LC] = blk.astype(BF16)
        lr, li = rows(pr, range(1, S5_L + 1)), rows(pi, range(1, S5_L + 1))
        vt[gl] = jnp.concatenate([cr * lr - ci * li, -(cr * li + ci * lr)],
                                 axis=1).astype(BF16)
        lr, li = rows(pr, range(S5_L - 1, -1, -1)), rows(pi, range(S5_L - 1, -1, -1))
        tw[gl, :, S5_LC:] = jnp.concatenate([lr * br - li * bi, lr * bi + li * br],
                                            axis=1).astype(BF16)


def _s5_weights(pwr, pwi, sqr, sqi, nsqi, bbr, bbi, c_re, c_im):
    nl = pwr.shape[0]
    g, n, c, ln = SSM_GROUPS, SSM_STATE, SSM_GROUP, S5_L
    by_group = lambda p: jnp.transpose(p, (0, 2, 1, 3))
    btr, bti = bbr.reshape(nl, g, c, n), bbi.reshape(nl, g, c, n)
    spec = lambda a, b: pl.BlockSpec((None, S5_GB, a, b), lambda l, q: (l, q, 0, 0))
    shp = lambda a, b: jax.ShapeDtypeStruct((nl, g, a, b), BF16)
    tw, vt = pl.pallas_call(
        _s5_toeplitz_kernel,
        grid=(nl, g // S5_GB),
        in_specs=[spec(ln + 1, n)] * 2 + [spec(c, n)] * 4,
        out_specs=[spec(S5_LC, S5_LC + 2 * n), spec(S5_LC, 2 * n)],
        out_shape=[shp(S5_LC, S5_LC + 2 * n), shp(S5_LC, 2 * n)],
        name="s5_toeplitz",
    )(by_group(pwr), by_group(pwi), btr, bti, c_re, c_im)
    pair = lambda a, b: jnp.concatenate([by_group(a), by_group(b)], axis=-1)
    sc = jnp.concatenate([pair(sqr, sqr), pair(nsqi, sqi)], axis=-1)
    return tw, vt, sc


def _granule_transpose(xs):
    xs = list(xs)
    lane = lax.broadcasted_iota(jnp.int32, xs[0].shape, 1)
    gshift = SSM_GROUP.bit_length() - 1
    for m in range(S5_GB.bit_length() - 1):
        d = SSM_GROUP << m
        upper = ((lane >> (gshift + m)) & 1) == 1
        nxt = list(xs)
        for ia in range(S5_GB):
            if (ia >> m) & 1:
                continue
            ib = ia | (1 << m)
            a, b = xs[ia], xs[ib]
            nxt[ia] = jnp.where(upper, pltpu.roll(b, d, axis=1), a)
            nxt[ib] = jnp.where(upper, b, pltpu.roll(a, LANES - d, axis=1))
        xs = nxt
    return xs


def _s5_kernel(u_ref, tw_ref, v_ref, sc_ref, d_ref, y_ref, carry, *, ts):
    t = pl.program_id(1)
    i = pl.program_id(2)
    r = ts // S5_L
    half = SSM_STATE

    @pl.when(t == 0)
    def _():
        carry[pl.ds(i * S5_GB, S5_GB)] = jnp.zeros((S5_GB, SUBLANES, LANES), F32)

    a_rows = [u_ref[pl.ds(s, r, stride=S5_L), :] for s in range(S5_L)]
    folded = [_granule_transpose(a_rows[h * S5_GB:(h + 1) * S5_GB]) for h in range(S5_L // S5_GB)]
    row = lax.broadcasted_iota(jnp.int32, (r, LANES), 0)
    row8 = lax.broadcasted_iota(jnp.int32, (SUBLANES, LANES), 0)
    groups = range(S5_GB)
    txs = [jnp.dot(jnp.concatenate([f[gl] for f in folded], axis=1).astype(BF16), tw_ref[gl],
                   preferred_element_type=F32) for gl in groups]
    scs = [sc_ref[gl] for gl in groups]
    c8s = [carry[i * S5_GB + gl] for gl in groups]
    hs_ = []
    for tx, sc, c8 in zip(txs, scs, c8s):
        x = tx[:, S5_LC:]
        inj = sc[0:1, :LANES] * c8 + sc[0:1, LANES:] * pltpu.roll(c8, half, axis=1)
        hs_.append(jnp.concatenate([x[:SUBLANES] + jnp.where(row8 == 0, inj, 0.0), x[SUBLANES:]], axis=0))
    for j in range(r.bit_length() - 1):
        sh = 1 << j
        shifted = [jnp.where(row >= sh, pltpu.roll(h, sh, axis=0), 0.0) for h in hs_]
        hs_ = [h + sc[j:j + 1, :LANES] * s + sc[j:j + 1, LANES:] * pltpu.roll(s, half, axis=1)
               for h, s, sc in zip(hs_, shifted, scs)]
    y_groups = []
    for gl, (tx, h, c8) in enumerate(zip(txs, hs_, c8s)):
        h_prev = jnp.where(row >= 1, pltpu.roll(h, 1, axis=0), jnp.broadcast_to(c8[0:1], (r, LANES)))
        y_groups.append(tx[:, :S5_LC] + _mm_nt(h_prev, v_ref[gl]))
        carry[i * S5_GB + gl] = jnp.broadcast_to(h[r - 1:r], (SUBLANES, LANES))
    d_skip = d_ref[...]
    for h in range(S5_L // S5_GB):
        unfolded = _granule_transpose([y[:, h * LANES:(h + 1) * LANES] for y in y_groups])
        for q in range(S5_GB):
            s = h * S5_GB + q
            y_ref[pl.ds(s, r, stride=S5_L), :] = unfolded[q] + d_skip * a_rows[s]


def _s5(proj, tw, v, sc, d_skip, li, bsz, seq, ts):
    nt = seq // ts
    ncb = D_MODEL // LANES
    grp = lambda a, b: pl.BlockSpec((None, S5_GB, a, b), lambda b_, t, i: (li, i, 0, 0))
    return pl.pallas_call(
        functools.partial(_s5_kernel, ts=ts),
        grid=(bsz, nt, ncb),
        in_specs=[
            pl.BlockSpec((ts, LANES), lambda b, t, i: (b * nt + t, A_U * ncb + i)),
            grp(S5_LC, S5_LC + 2 * SSM_STATE),
            grp(S5_LC, 2 * SSM_STATE),
            grp(S5_NPOW, 2 * LANES),
            pl.BlockSpec((None, 1, LANES), lambda b, t, i: (li, 0, i)),
        ],
        out_specs=pl.BlockSpec((ts, LANES), lambda b, t, i: (b * nt + t, i)),
        out_shape=jax.ShapeDtypeStruct((bsz * seq, D_MODEL), F32),
        scratch_shapes=[pltpu.VMEM((SSM_GROUPS, SUBLANES, LANES), F32)],
        compiler_params=pltpu.CompilerParams(
            dimension_semantics=("arbitrary", "arbitrary", "arbitrary"), vmem_limit_bytes=VMEM_LIMIT),
        name="s5",
    )(proj, tw, v, sc, d_skip)


def _merge_kernel(ya_ref, ys_ref, zb_ref, rb_ref, x_ref, wglu_ref, bglu_ref, wout_ref, np_ref, o_ref):
    y = _gelu_tanh(ys_ref[...])
    y = y * _sigmoid(jnp.dot(y.astype(BF16), wglu_ref[...], preferred_element_type=F32) + bglu_ref[...])
    y_b = y * _silu(zb_ref[...])
    merged = ya_ref[...] + _sigmoid(rb_ref[...]) * y_b
    out = jnp.dot(merged.astype(BF16), wout_ref[...], preferred_element_type=F32)
    out = out * lax.rsqrt(jnp.mean(out * out, axis=-1, keepdims=True) + EPS) * np_ref[...]
    o_ref[...] = x_ref[...] + out


def _merge(y_a, y_s, proj_b, x2, w_glu, b_glu, w_out, norm_post, li, tm):
    t = x2.shape[0]
    col = lambda c: pl.BlockSpec((tm, D_MODEL), lambda i: (i, c))
    vec = pl.BlockSpec((None, 1, D_MODEL), lambda i: (li, 0, 0))
    mat = pl.BlockSpec((None, D_MODEL, D_MODEL), lambda i: (li, 0, 0))
    return pl.pallas_call(
        _merge_kernel,
        grid=(t // tm,),
        in_specs=[col(0), col(0), col(B_ZB), col(B_RB), col(0), mat, vec, mat, vec],
        out_specs=col(0),
        out_shape=jax.ShapeDtypeStruct((t, D_MODEL), F32),
        compiler_params=pltpu.CompilerParams(
            dimension_semantics=("arbitrary",), vmem_limit_bytes=VMEM_LIMIT),
        name="merge",
    )(y_a, y_s, proj_b, proj_b, x2, w_glu, b_glu, w_out, norm_post)


def _prep_w_in(w_in):
    w_in = w_in.astype(BF16)
    o_bd = 4 * DN_WIDTH
    o_u = o_bd + 2 * DN_HEADS
    w_a = jnp.concatenate([w_in[..., :o_bd], w_in[..., o_u:o_u + D_MODEL]], axis=-1)
    pad = jnp.zeros(w_in.shape[:-1] + (LANES - 2 * DN_HEADS,), w_in.dtype)
    w_b = jnp.concatenate([w_in[..., o_u + D_MODEL:], w_in[..., o_bd:o_u], pad], axis=-1)
    return w_a, w_b


def _pad_rows(a, rows=SUBLANES):
    return jnp.pad(a, ((0, 0), (0, rows - a.shape[1]), (0, 0)))


def _trunk(x, norm_pre, w_in, conv_w, a_log, dt_bias, head_norm, ssm_a_re, ssm_a_im, ssm_log_dt,
           ssm_b_re, ssm_b_im, ssm_c_re, ssm_c_im, ssm_d, w_glu, b_glu, w_out, norm_post,
           *, tm=256, tm_merge=512, ts_dn=512, bt_dn=128, dn_heads=8, ts_s5=4096):
    bsz, seq, _ = x.shape
    depth = w_in.shape[0]
    ts_s5 = min(ts_s5, seq)
    x2 = x.reshape(bsz * seq, D_MODEL)
    w_a, w_b = _prep_w_in(w_in)
    w_glu_b = w_glu.astype(BF16)
    w_out_b = w_out.astype(BF16)
    s5_tw, s5_v, s5_sc = _s5_weights(*_s5_params(ssm_a_re, ssm_a_im, ssm_log_dt, ssm_b_re, ssm_b_im),
                                     ssm_c_re, ssm_c_im)
    at_decay = lambda p: jnp.pad(p, ((0, 0), (DECAY_LANE0, LANES - DECAY_LANE0 - DN_HEADS)))
    gate_p = _pad_rows(jnp.stack([at_decay(a_log), at_decay(dt_bias)], axis=1))
    conv_w8 = _pad_rows(conv_w)
    head_n = _pad_rows(head_norm[:, None, :])
    row = lambda p: p[:, None, :]
    for li in range(depth):
        proj_a, proj_b = _in_proj(x2, row(norm_pre), w_a, w_b, conv_w8, gate_p, li, seq, tm)
        y_a = _deltanet(proj_a, proj_b, head_n, li, bsz, seq, ts_dn, bt_dn, dn_heads)
        y_s = _s5(proj_a, s5_tw, s5_v, s5_sc, row(ssm_d), li, bsz, seq, ts_s5)
        x2 = _merge(y_a, y_s, proj_b, x2, w_glu_b, row(b_glu), w_out_b, row(norm_post), li, tm_merge)
    return x2.reshape(bsz, seq, D_MODEL)


def kernel(x, norm_pre, w_in, conv_w, a_log, dt_bias, head_norm, ssm_a_re, ssm_a_im, ssm_log_dt,
           ssm_b_re, ssm_b_im, ssm_c_re, ssm_c_im, ssm_d, w_glu, b_glu, w_out, norm_post):
    return _trunk(x, norm_pre, w_in, conv_w, a_log, dt_bias, head_norm, ssm_a_re, ssm_a_im, ssm_log_dt,
                  ssm_b_re, ssm_b_im, ssm_c_re, ssm_c_im, ssm_d, w_glu, b_glu, w_out, norm_post)
```

```python
import functools

import jax
import jax.numpy as jnp
from jax import lax
from jax.experimental import pallas as pl
from jax.experimental.pallas import tpu as pltpu

D_MODEL = 1024
DN_HEADS = 8
DN_HEAD_DIM = 128
DN_WIDTH = DN_HEADS * DN_HEAD_DIM
CONV_K = 4
CHUNK = 64
SSM_GROUP = 16
SSM_GROUPS = 64
SSM_STATE = 64
EPS = 1e-6

LANES = 128
SUBLANES = 8
V7X_VMEM_BYTES = 64 * 1024 * 1024
VMEM_LIMIT = V7X_VMEM_BYTES * 7 // 8

PROJ_A_W = 5 * D_MODEL
PROJ_B_W = 3 * D_MODEL + LANES
A_ZA, A_U = 3, 4
B_ZB, B_RA, B_RB = 0, 1, 2
PROJ_B_BD = 3 * D_MODEL // LANES
DECAY_LANE0 = DN_HEADS
GATE_BLOCK = 256

S5_L = 16
S5_LC = S5_L * SSM_GROUP
S5_GB = LANES // SSM_GROUP
S5_NPOW = 8

F32 = jnp.float32
BF16 = jnp.bfloat16
HIGHEST = lax.Precision.HIGHEST


def _sigmoid(x):
    return 0.5 * jnp.tanh(0.5 * x) + 0.5


def _silu(x):
    hx = 0.5 * x
    return hx * jnp.tanh(hx) + hx


def _gelu_tanh(x):
    c = 0.7978845608028654
    return 0.5 * x * (1.0 + jnp.tanh(c * (x + 0.044715 * (x * x * x))))


def _softplus(x):
    return jnp.maximum(x, 0.0) + jnp.log(1.0 + jnp.exp(-jnp.abs(x)))


def _mm(a, b):
    return jnp.dot(a.astype(BF16), b.astype(BF16), preferred_element_type=F32)


def _mm_nt(a, b):
    return lax.dot_general(a.astype(BF16), b.astype(BF16), (((1,), (1,)), ((), ())),
                           preferred_element_type=F32)


def _mm_tn(a, b):
    return lax.dot_general(a.astype(BF16), b.astype(BF16), (((0,), (0,)), ((), ())),
                           preferred_element_type=F32)


def _mm_f32(a, b):
    return jnp.dot(a, b, precision=HIGHEST, preferred_element_type=F32)


def _mm_nt_f32(a, b):
    return lax.dot_general(a, b, (((1,), (1,)), ((), ())), precision=HIGHEST,
                           preferred_element_type=F32)


def _in_proj_kernel(x_ref, g_ref, wa_ref, wb_ref, cw_ref, gp_ref, pa_ref, pb_ref, halo, *, tm, tiles_per_seq):
    i = pl.program_id(0)

    @pl.when(i % tiles_per_seq == 0)
    def _():
        halo[...] = jnp.zeros_like(halo)

    x = x_ref[...]
    h = x * lax.rsqrt(jnp.mean(x * x, axis=-1, keepdims=True) + EPS) * g_ref[...]
    hb = h.astype(BF16)
    bd = jnp.dot(hb, wb_ref[:, PROJ_B_BD * LANES:], preferred_element_type=F32)
    g = -jnp.exp(gp_ref[0:1, :]) * _softplus(bd + gp_ref[1:2, :])
    sb = min(tm, GATE_BLOCK)
    row = lax.broadcasted_iota(jnp.int32, (sb, sb), 0)
    col = lax.broadcasted_iota(jnp.int32, (sb, sb), 1)
    shift = CHUNK.bit_length() - 1
    tril = (((row >> shift) == (col >> shift)) & (row >= col)).astype(F32)
    g_cum = jnp.concatenate([_mm_f32(tril, g[r0:r0 + sb, :]) for r0 in range(0, tm, sb)], axis=0)
    lane = lax.broadcasted_iota(jnp.int32, (tm, LANES), 1)
    pb_ref[:, PROJ_B_BD * LANES:] = jnp.where(lane < DECAY_LANE0, _sigmoid(bd), g_cum)

    wins = []
    for kind in range(3):
        a = kind * DN_WIDTH
        raw = jnp.dot(hb, wa_ref[:, a:a + DN_WIDTH], preferred_element_type=F32)
        wins.append(jnp.concatenate([halo[kind], raw], axis=0))
        halo[kind] = raw[tm - SUBLANES:, :]
    for a in range(3 * DN_WIDTH, PROJ_A_W, D_MODEL):
        pa_ref[:, a:a + D_MODEL] = jnp.dot(hb, wa_ref[:, a:a + D_MODEL], preferred_element_type=F32)
    for a in range(0, PROJ_B_BD * LANES, D_MODEL):
        pb_ref[:, a:a + D_MODEL] = jnp.dot(hb, wb_ref[:, a:a + D_MODEL], preferred_element_type=F32)

    scale = DN_HEAD_DIM ** -0.5
    for kind in range(3):
        a = kind * DN_WIDTH
        ws = wins[kind]
        w = cw_ref[:, a:a + DN_WIDTH]
        acc = ws[SUBLANES:] * w[CONV_K - 1:CONV_K]
        for j in range(CONV_K - 1):
            acc = acc + pltpu.roll(ws, CONV_K - 1 - j, axis=0)[SUBLANES:] * w[j:j + 1]
        y = _silu(acc)
        for hl in range(DN_HEADS):
            lo = hl * LANES
            yh = y[:, lo:lo + LANES]
            if kind == 0:
                yh = yh * (lax.rsqrt(jnp.sum(yh * yh, axis=-1, keepdims=True) + EPS) * scale)
            elif kind == 1:
                yh = yh * lax.rsqrt(jnp.sum(yh * yh, axis=-1, keepdims=True) + EPS)
            pa_ref[:, a + lo:a + lo + LANES] = yh


def _in_proj(x2, gain, w_a, w_b, conv_w8, gate_p, li, seq, tm):
    t = x2.shape[0]
    return pl.pallas_call(
        functools.partial(_in_proj_kernel, tm=tm, tiles_per_seq=seq // tm),
        grid=(t // tm,),
        in_specs=[
            pl.BlockSpec((tm, D_MODEL), lambda i: (i, 0)),
            pl.BlockSpec((None, 1, D_MODEL), lambda i: (li, 0, 0)),
            pl.BlockSpec((None, D_MODEL, PROJ_A_W), lambda i: (li, 0, 0), pipeline_mode=pl.Buffered(1)),
            pl.BlockSpec((None, D_MODEL, PROJ_B_W), lambda i: (li, 0, 0), pipeline_mode=pl.Buffered(1)),
            pl.BlockSpec((None, SUBLANES, 3 * DN_WIDTH), lambda i: (li, 0, 0)),
            pl.BlockSpec((None, SUBLANES, LANES), lambda i: (li, 0, 0)),
        ],
        out_specs=[
            pl.BlockSpec((tm, PROJ_A_W), lambda i: (i, 0)),
            pl.BlockSpec((tm, PROJ_B_W), lambda i: (i, 0)),
        ],
        out_shape=[
            jax.ShapeDtypeStruct((t, PROJ_A_W), F32),
            jax.ShapeDtypeStruct((t, PROJ_B_W), F32),
        ],
        scratch_shapes=[pltpu.VMEM((3, SUBLANES, DN_WIDTH), F32)],
        compiler_params=pltpu.CompilerParams(
            dimension_semantics=("arbitrary",), vmem_limit_bytes=VMEM_LIMIT),
        name="in_proj",
    )(x2, gain, w_a, w_b, conv_w8, gate_p)


def _neumann_inverse(l_mats, block):
    n = l_mats[0].shape[0]
    row = lax.broadcasted_iota(jnp.int32, (n, n), 0)
    col = lax.broadcasted_iota(jnp.int32, (n, n), 1)
    eye = jnp.where(row == col, 1.0, 0.0)
    ms = [-l for l in l_mats]
    ps = [eye + m for m in ms]
    for _ in range((block - 1).bit_length() - 1):
        mbs = [m.astype(BF16) for m in ms]
        ms = [jnp.dot(mb, mb, preferred_element_type=F32) for mb in mbs]
        ps = [p + _mm(p, m) for p, m in zip(ps, ms)]
    return ps


def _dn_kernel(q_ref, k_ref, v_ref, bd_ref, za_ref, ra_ref, hn_ref, o_ref, state, *, ts, bt, heads):
    hb = pl.program_id(1)
    t = pl.program_id(2)
    nc = bt // CHUNK

    @pl.when(t == 0)
    def _():
        state[...] = jnp.zeros_like(state)

    head_gain = hn_ref[0:1, :]
    lane = lax.broadcasted_iota(jnp.int32, (bt, LANES), 1)
    sub = lax.broadcasted_iota(jnp.int32, (LANES, bt), 0)
    row = lax.broadcasted_iota(jnp.int32, (bt, bt), 0)
    col = lax.broadcasted_iota(jnp.int32, (bt, bt), 1)
    shift = CHUNK.bit_length() - 1
    same = (row >> shift) == (col >> shift)
    causal = same & (row >= col)
    strict = same & (row > col)

    units = [(st, hl) for st in range(ts // bt) for hl in range(heads)]
    gates = []
    for st in range(ts // bt):
        bd = bd_ref[st * bt:(st + 1) * bt, :]
        gates.append((bd, jnp.transpose(bd)))

    qn, kn, vv, beta, g_col, decay = [], [], [], [], [], []
    for st, hl in units:
        tok = slice(st * bt, (st + 1) * bt)
        qn.append(q_ref[tok, hl * LANES:(hl + 1) * LANES])
        kn.append(k_ref[tok, hl * LANES:(hl + 1) * LANES])
        vv.append(v_ref[tok, hl * LANES:(hl + 1) * LANES])
        hg = hb * heads + hl
        bd, bd_t = gates[st]
        beta.append(jnp.sum(jnp.where(lane == hg, bd, 0.0), axis=-1, keepdims=True))
        gc = jnp.sum(jnp.where(lane == hg + DECAY_LANE0, bd, 0.0), axis=-1, keepdims=True)
        gr = jnp.sum(jnp.where(sub == hg + DECAY_LANE0, bd_t, 0.0), axis=0, keepdims=True)
        g_col.append(gc)
        decay.append(jnp.exp(jnp.where(causal, gc - gr, -jnp.inf)))

    kb = [k * b for k, b in zip(kn, beta)]
    knb = [k.astype(BF16) for k in kn]
    gram = [_mm_nt(jnp.concatenate([a, q], axis=0), b) for a, q, b in zip(kb, qn, knb)]
    l_mats = [jnp.where(strict, gm[:bt] * d, 0.0) for gm, d in zip(gram, decay)]
    a_qk = [gm[bt:] * d for gm, d in zip(gram, decay)]
    t_inv = _neumann_inverse(l_mats, CHUNK)
    e_g = [jnp.exp(g) for g in g_col]
    sol = [_mm(ti, jnp.concatenate([v * b, k * e], axis=-1))
           for ti, v, b, k, e in zip(t_inv, vv, beta, kb, e_g)]
    a_sol = [_mm(a, s) for a, s in zip(a_qk, sol)]
    q_eff = [q * e - a[:, DN_HEAD_DIM:] for q, e, a in zip(qn, e_g, a_sol)]
    g_last = [jnp.concatenate(
        [jnp.broadcast_to(g[(c + 1) * CHUNK - 1:(c + 1) * CHUNK, :], (CHUNK, 1)) for c in range(nc)], axis=0)
        for g in g_col]
    k_dec = [k * jnp.exp(gl - g) for k, gl, g in zip(kn, g_last, g_col)]
    gamma = [jnp.exp(gl) for gl in g_last]
    kt_sol = [[_mm_tn(kd[c * CHUNK:(c + 1) * CHUNK, :], s[c * CHUNK:(c + 1) * CHUNK, :]) for c in range(nc)]
              for kd, s in zip(k_dec, sol)]

    s_list = [state[hl] for hl in range(heads)]
    for st in range(ts // bt):
        for c in range(nc):
            rows = slice(c * CHUNK, (c + 1) * CHUNK)
            for hl in range(heads):
                i = st * heads + hl
                s = s_list[hl]
                on_s = _mm(jnp.concatenate([q_eff[i][rows, :], kt_sol[i][c][:, DN_HEAD_DIM:]], axis=0), s)
                o = on_s[:CHUNK] + a_sol[i][rows, :DN_HEAD_DIM]
                s_list[hl] = s * gamma[i][c * CHUNK:c * CHUNK + 1, :] - on_s[CHUNK:] + kt_sol[i][c][:, :DN_HEAD_DIM]
                o = o * lax.rsqrt(jnp.mean(o * o, axis=-1, keepdims=True) + EPS) * head_gain
                tok = slice(st * bt + c * CHUNK, st * bt + (c + 1) * CHUNK)
                cols = slice(hl * LANES, (hl + 1) * LANES)
                o_ref[tok, cols] = _sigmoid(ra_ref[tok, cols]) * (o * _silu(za_ref[tok, cols]))

    for hl in range(heads):
        state[hl] = s_list[hl]


def _deltanet(proj_a, proj_b, head_n, li, bsz, seq, ts, bt, heads):
    nt = seq // ts
    wid = heads * LANES
    nqb = DN_WIDTH // wid
    tok = lambda b, h, t: b * nt + t
    return pl.pallas_call(
        functools.partial(_dn_kernel, ts=ts, bt=bt, heads=heads),
        grid=(bsz, DN_HEADS // heads, nt),
        in_specs=[
            pl.BlockSpec((ts, wid), lambda b, h, t: (tok(b, h, t), h)),
            pl.BlockSpec((ts, wid), lambda b, h, t: (tok(b, h, t), nqb + h)),
            pl.BlockSpec((ts, wid), lambda b, h, t: (tok(b, h, t), 2 * nqb + h)),
            pl.BlockSpec((ts, LANES), lambda b, h, t: (tok(b, h, t), PROJ_B_BD)),
            pl.BlockSpec((ts, wid), lambda b, h, t: (tok(b, h, t), A_ZA * nqb + h)),
            pl.BlockSpec((ts, wid), lambda b, h, t: (tok(b, h, t), B_RA * nqb + h)),
            pl.BlockSpec((None, SUBLANES, LANES), lambda b, h, t: (li, 0, 0)),
        ],
        out_specs=pl.BlockSpec((ts, wid), lambda b, h, t: (tok(b, h, t), h)),
        out_shape=jax.ShapeDtypeStruct((bsz * seq, DN_WIDTH), F32),
        scratch_shapes=[pltpu.VMEM((heads, DN_HEAD_DIM, DN_HEAD_DIM), F32)],
        compiler_params=pltpu.CompilerParams(
            dimension_semantics=("arbitrary", "arbitrary", "arbitrary"), vmem_limit_bytes=VMEM_LIMIT),
        name="deltanet",
    )(proj_a, proj_a, proj_a, proj_b, proj_a, proj_b, head_n)


def _s5_param_kernel(are, aim, ldt, arer, aimr, ldtr, brt, bit, pwr, pwi, sqr, sqi, nsqi, bbr, bbi):
    ar = are[0]
    ai = aim[0]
    dt = jnp.exp(ldt[0])
    for k in range(S5_L + 1):
        mag = jnp.exp(ar * dt * float(k))
        ang = ai * dt * float(k)
        pwr[0, k] = mag * jnp.cos(ang)
        pwi[0, k] = mag * jnp.sin(ang)
    pr = pwr[0, S5_L]
    pi = pwi[0, S5_L]
    for j in range(S5_NPOW):
        sqr[0, j] = pr
        sqi[0, j] = pi
        nsqi[0, j] = -pi
        pr, pi = pr * pr - pi * pi, 2.0 * (pr * pi)
    ar = arer[0]
    ai = aimr[0]
    dt = jnp.exp(ldtr[0])
    mag = jnp.exp(ar * dt)
    lr = mag * jnp.cos(ai * dt)
    li = mag * jnp.sin(ai * dt)
    den = ar * ar + ai * ai
    fr = ((lr - 1.0) * ar + li * ai) / den
    fi = (li * ar - (lr - 1.0) * ai) / den
    br = brt[0]
    bi = bit[0]
    bbr[0] = fr * br - fi * bi
    bbi[0] = fr * bi + fi * br


def _s5_params(a_re, a_im, log_dt, b_re, b_im):
    nl = a_re.shape[0]
    g, n, c = SSM_GROUPS, SSM_STATE, SSM_GROUP
    ldt = jnp.broadcast_to(log_dt[..., None], (nl, g, n))
    rep = lambda a: jnp.repeat(a, c, axis=1)
    brt = jnp.swapaxes(b_re, 2, 3).reshape(nl, g * c, n)
    bit = jnp.swapaxes(b_im, 2, 3).reshape(nl, g * c, n)
    small = pl.BlockSpec((1, g, n), lambda l: (l, 0, 0))
    big = pl.BlockSpec((1, g * c, n), lambda l: (l, 0, 0))
    powr = pl.BlockSpec((1, S5_L + 1, g, n), lambda l: (l, 0, 0, 0))
    sqs = pl.BlockSpec((1, S5_NPOW, g, n), lambda l: (l, 0, 0, 0))
    pw_shape = jax.ShapeDtypeStruct((nl, S5_L + 1, g, n), F32)
    sq_shape = jax.ShapeDtypeStruct((nl, S5_NPOW, g, n), F32)
    bb_shape = jax.ShapeDtypeStruct((nl, g * c, n), F32)
    return pl.pallas_call(
        _s5_param_kernel,
        grid=(nl,),
        in_specs=[small, small, small, big, big, big, big, big],
        out_specs=[powr, powr, sqs, sqs, sqs, big, big],
        out_shape=[pw_shape, pw_shape, sq_shape, sq_shape, sq_shape, bb_shape, bb_shape],
        name="s5_params",
    )(a_re, a_im, ldt, rep(a_re), rep(a_im), rep(ldt), brt, bit)


def _s5_toeplitz_kernel(pwr, pwi, btr, bti, cr_ref, ci_ref, tw, vt):
    shape = (SSM_GROUP, SSM_STATE)
    lane = lax.broadcasted_iota(jnp.int32, (SSM_GROUP, S5_LC), 1)
    for gl in range(S5_GB):
        pr = pwr[gl]
        pi = pwi[gl]
        rows = lambda p, ms: jnp.concatenate([jnp.broadcast_to(p[m:m + 1, :], shape) for m in ms], axis=0)
        tile = lambda a: jnp.concatenate([a] * S5_L, axis=0)
        cr, ci = tile(cr_ref[gl]), tile(ci_ref[gl])
        br, bi = tile(btr[gl]), tile(bti[gl])
        lr, li = rows(pr, range(S5_L)), rows(pi, range(S5_L))
        qr = cr * lr - ci * li
        qi = cr * li + ci * lr
        kall = _mm_nt_f32(btr[gl], qr) - _mm_nt_f32(bti[gl], qi)
        for s in range(S5_L):
            off = s * SSM_GROUP
            blk = kall if s == 0 else jnp.where(lane >= off, pltpu.roll(kall, off, axis=1), 0.0)
            tw[gl, off:off + SSM_GROUP, 0:S5_LC] = blk.astype(BF16)
        lr, li = rows(pr, range(1, S5_L + 1)), rows(pi, range(1, S5_L + 1))
        vt[gl] = jnp.concatenate([cr * lr - ci * li, -(cr * li + ci * lr)],
                                 axis=1).astype(BF16)
        lr, li = rows(pr, range(S5_L - 1, -1, -1)), rows(pi, range(S5_L - 1, -1, -1))
        tw[gl, :, S5_LC:] = jnp.concatenate([lr * br - li * bi, lr * bi + li * br],
                                            axis=1).astype(BF16)


def _s5_weights(pwr, pwi, sqr, sqi, nsqi, bbr, bbi, c_re, c_im):
    nl = pwr.shape[0]
    g, n, c, ln = SSM_GROUPS, SSM_STATE, SSM_GROUP, S5_L
    by_group = lambda p: jnp.transpose(p, (0, 2, 1, 3))
    btr, bti = bbr.reshape(nl, g, c, n), bbi.reshape(nl, g, c, n)
    spec = lambda a, b: pl.BlockSpec((None, S5_GB, a, b), lambda l, q: (l, q, 0, 0))
    shp = lambda a, b: jax.ShapeDtypeStruct((nl, g, a, b), BF16)
    tw, vt = pl.pallas_call(
        _s5_toeplitz_kernel,
        grid=(nl, g // S5_GB),
        in_specs=[spec(ln + 1, n)] * 2 + [spec(c, n)] * 4,
        out_specs=[spec(S5_LC, S5_LC + 2 * n), spec(S5_LC, 2 * n)],
        out_shape=[shp(S5_LC, S5_LC + 2 * n), shp(S5_LC, 2 * n)],
        name="s5_toeplitz",
    )(by_group(pwr), by_group(pwi), btr, bti, c_re, c_im)
    pair = lambda a, b: jnp.concatenate([by_group(a), by_group(b)], axis=-1)
    sc = jnp.concatenate([pair(sqr, sqr), pair(nsqi, sqi)], axis=-1)
    return tw, vt, sc


def _granule_transpose(xs):
    xs = list(xs)
    lane = lax.broadcasted_iota(jnp.int32, xs[0].shape, 1)
    gshift = SSM_GROUP.bit_length() - 1
    for m in range(S5_GB.bit_length() - 1):
        d = SSM_GROUP << m
        upper = ((lane >> (gshift + m)) & 1) == 1
        nxt = list(xs)
        for ia in range(S5_GB):
            if (ia >> m) & 1:
                continue
            ib = ia | (1 << m)
            a, b = xs[ia], xs[ib]
            nxt[ia] = jnp.where(upper, pltpu.roll(b, d, axis=1), a)
            nxt[ib] = jnp.where(upper, b, pltpu.roll(a, LANES - d, axis=1))
        xs = nxt
    return xs


def _s5_kernel(u_ref, tw_ref, v_ref, sc_ref, d_ref, y_ref, carry, *, ts):
    t = pl.program_id(1)
    i = pl.program_id(2)
    r = ts // S5_L
    half = SSM_STATE

    @pl.when(t == 0)
    def _():
        carry[pl.ds(i * S5_GB, S5_GB)] = jnp.zeros((S5_GB, SUBLANES, LANES), F32)

    a_rows = [u_ref[pl.ds(s, r, stride=S5_L), :] for s in range(S5_L)]
    folded = [_granule_transpose(a_rows[h * S5_GB:(h + 1) * S5_GB]) for h in range(S5_L // S5_GB)]
    row = lax.broadcasted_iota(jnp.int32, (r, LANES), 0)
    row8 = lax.broadcasted_iota(jnp.int32, (SUBLANES, LANES), 0)
    groups = range(S5_GB)
    txs = [jnp.dot(jnp.concatenate([f[gl] for f in folded], axis=1).astype(BF16), tw_ref[gl],
                   preferred_element_type=F32) for gl in groups]
    scs = [sc_ref[gl] for gl in groups]
    c8s = [carry[i * S5_GB + gl] for gl in groups]
    hs_ = []
    for tx, sc, c8 in zip(txs, scs, c8s):
        x = tx[:, S5_LC:]
        inj = sc[0:1, :LANES] * c8 + sc[0:1, LANES:] * pltpu.roll(c8, half, axis=1)
        hs_.append(jnp.concatenate([x[:SUBLANES] + jnp.where(row8 == 0, inj, 0.0), x[SUBLANES:]], axis=0))
    for j in range(r.bit_length() - 1):
        sh = 1 << j
        shifted = [jnp.where(row >= sh, pltpu.roll(h, sh, axis=0), 0.0) for h in hs_]
        hs_ = [h + sc[j:j + 1, :LANES] * s + sc[j:j + 1, LANES:] * pltpu.roll(s, half, axis=1)
               for h, s, sc in zip(hs_, shifted, scs)]
    y_groups = []
    for gl, (tx, h, c8) in enumerate(zip(txs, hs_, c8s)):
        h_prev = jnp.where(row >= 1, pltpu.roll(h, 1, axis=0), jnp.broadcast_to(c8[0:1], (r, LANES)))
        y_groups.append(tx[:, :S5_LC] + _mm_nt(h_prev, v_ref[gl]))
        carry[i * S5_GB + gl] = jnp.broadcast_to(h[r - 1:r], (SUBLANES, LANES))
    d_skip = d_ref[...]
    for h in range(S5_L // S5_GB):
        unfolded = _granule_transpose([y[:, h * LANES:(h + 1) * LANES] for y in y_groups])
        for q in range(S5_GB):
            s = h * S5_GB + q
            y_ref[pl.ds(s, r, stride=S5_L), :] = unfolded[q] + d_skip * a_rows[s]


def _s5(proj, tw, v, sc, d_skip, li, bsz, seq, ts):
    nt = seq // ts
    ncb = D_MODEL // LANES
    grp = lambda a, b: pl.BlockSpec((None, S5_GB, a, b), lambda b_, t, i: (li, i, 0, 0))
    return pl.pallas_call(
        functools.partial(_s5_kernel, ts=ts),
        grid=(bsz, nt, ncb),
        in_specs=[
            pl.BlockSpec((ts, LANES), lambda b, t, i: (b * nt + t, A_U * ncb + i)),
            grp(S5_LC, S5_LC + 2 * SSM_STATE),
            grp(S5_LC, 2 * SSM_STATE),
            grp(S5_NPOW, 2 * LANES),
            pl.BlockSpec((None, 1, LANES), lambda b, t, i: (li, 0, i)),
        ],
        out_specs=pl.BlockSpec((ts, LANES), lambda b, t, i: (b * nt + t, i)),
        out_shape=jax.ShapeDtypeStruct((bsz * seq, D_MODEL), F32),
        scratch_shapes=[pltpu.VMEM((SSM_GROUPS, SUBLANES, LANES), F32)],
        compiler_params=pltpu.CompilerParams(
            dimension_semantics=("arbitrary", "arbitrary", "arbitrary"), vmem_limit_bytes=VMEM_LIMIT),
        name="s5",
    )(proj, tw, v, sc, d_skip)


def _merge_kernel(ya_ref, ys_ref, zb_ref, rb_ref, x_ref, wglu_ref, bglu_ref, wout_ref, np_ref, o_ref):
    y = _gelu_tanh(ys_ref[...])
    y = y * _sigmoid(jnp.dot(y.astype(BF16), wglu_ref[...], preferred_element_type=F32) + bglu_ref[...])
    y_b = y * _silu(zb_ref[...])
    merged = ya_ref[...] + _sigmoid(rb_ref[...]) * y_b
    out = jnp.dot(merged.astype(BF16), wout_ref[...], preferred_element_type=F32)
    out = out * lax.rsqrt(jnp.mean(out * out, axis=-1, keepdims=True) + EPS) * np_ref[...]
    o_ref[...] = x_ref[...] + out


def _merge(y_a, y_s, proj_b, x2, w_glu, b_glu, w_out, norm_post, li, tm):
    t = x2.shape[0]
    col = lambda c: pl.BlockSpec((tm, D_MODEL), lambda i: (i, c))
    vec = pl.BlockSpec((None, 1, D_MODEL), lambda i: (li, 0, 0))
    mat = pl.BlockSpec((None, D_MODEL, D_MODEL), lambda i: (li, 0, 0))
    return pl.pallas_call(
        _merge_kernel,
        grid=(t // tm,),
        in_specs=[col(0), col(0), col(B_ZB), col(B_RB), col(0), mat, vec, mat, vec],
        out_specs=col(0),
        out_shape=jax.ShapeDtypeStruct((t, D_MODEL), F32),
        compiler_params=pltpu.CompilerParams(
            dimension_semantics=("arbitrary",), vmem_limit_bytes=VMEM_LIMIT),
        name="merge",
    )(y_a, y_s, proj_b, proj_b, x2, w_glu, b_glu, w_out, norm_post)


def _prep_w_in(w_in):
    w_in = w_in.astype(BF16)
    o_bd = 4 * DN_WIDTH
    o_u = o_bd + 2 * DN_HEADS
    w_a = jnp.concatenate([w_in[..., :o_bd], w_in[..., o_u:o_u + D_MODEL]], axis=-1)
    pad = jnp.zeros(w_in.shape[:-1] + (LANES - 2 * DN_HEADS,), w_in.dtype)
    w_b = jnp.concatenate([w_in[..., o_u + D_MODEL:], w_in[..., o_bd:o_u], pad], axis=-1)
    return w_a, w_b


def _pad_rows(a, rows=SUBLANES):
    return jnp.pad(a, ((0, 0), (0, rows - a.shape[1]), (0, 0)))


def _trunk(x, norm_pre, w_in, conv_w, a_log, dt_bias, head_norm, ssm_a_re, ssm_a_im, ssm_log_dt,
           ssm_b_re, ssm_b_im, ssm_c_re, ssm_c_im, ssm_d, w_glu, b_glu, w_out, norm_post,
           *, tm=256, tm_merge=512, ts_dn=512, bt_dn=128, dn_heads=8, ts_s5=4096):
    bsz, seq, _ = x.shape
    depth = w_in.shape[0]
    ts_s5 = min(ts_s5, seq)
    x2 = x.reshape(bsz * seq, D_MODEL)
    w_a, w_b = _prep_w_in(w_in)
    w_glu_b = w_glu.astype(BF16)
    w_out_b = w_out.astype(BF16)
    s5_tw, s5_v, s5_sc = _s5_weights(*_s5_params(ssm_a_re, ssm_a_im, ssm_log_dt, ssm_b_re, ssm_b_im),
                                     ssm_c_re, ssm_c_im)
    at_decay = lambda p: jnp.pad(p, ((0, 0), (DECAY_LANE0, LANES - DECAY_LANE0 - DN_HEADS)))
    gate_p = _pad_rows(jnp.stack([at_decay(a_log), at_decay(dt_bias)], axis=1))
    conv_w8 = _pad_rows(conv_w)
    head_n = _pad_rows(head_norm[:, None, :])
    row = lambda p: p[:, None, :]
    for li in range(depth):
        proj_a, proj_b = _in_proj(x2, row(norm_pre), w_a, w_b, conv_w8, gate_p, li, seq, tm)
        y_a = _deltanet(proj_a, proj_b, head_n, li, bsz, seq, ts_dn, bt_dn, dn_heads)
        y_s = _s5(proj_a, s5_tw, s5_v, s5_sc, row(ssm_d), li, bsz, seq, ts_s5)
        x2 = _merge(y_a, y_s, proj_b, x2, w_glu_b, row(b_glu), w_out_b, row(norm_post), li, tm_merge)
    return x2.reshape(bsz, seq, D_MODEL)


def kernel(x, norm_pre, w_in, conv_w, a_log, dt_bias, head_norm, ssm_a_re, ssm_a_im, ssm_log_dt,
           ssm_b_re, ssm_b_im, ssm_c_re, ssm_c_im, ssm_d, w_glu, b_glu, w_out, norm_post):
    return _trunk(x, norm_pre, w_in, conv_w, a_log, dt_bias, head_norm, ssm_a_re, ssm_a_im, ssm_log_dt,
                  ssm_b_re, ssm_b_im, ssm_c_re, ssm_c_im, ssm_d, w_glu, b_glu, w_out, norm_post)
```

```python
import functools

import jax
import jax.numpy as jnp
from jax import lax
from jax.experimental import pallas as pl
from jax.experimental.pallas import tpu as pltpu

D_MODEL = 1024
DN_HEADS = 8
DN_HEAD_DIM = 128
DN_WIDTH = DN_HEADS * DN_HEAD_DIM
CONV_K = 4
CHUNK = 64
SSM_GROUP = 16
SSM_GROUPS = 64
SSM_STATE = 64
EPS = 1e-6

LANES = 128
SUBLANES = 8
V7X_VMEM_BYTES = 64 * 1024 * 1024
VMEM_LIMIT = V7X_VMEM_BYTES * 7 // 8

PROJ_A_W = 5 * D_MODEL
PROJ_B_W = 3 * D_MODEL + LANES
A_ZA, A_U = 3, 4
B_ZB, B_RA, B_RB = 0, 1, 2
PROJ_B_BD = 3 * D_MODEL // LANES
DECAY_LANE0 = DN_HEADS
GATE_BLOCK = 256

S5_L = 16
S5_LC = S5_L * SSM_GROUP
S5_GB = LANES // SSM_GROUP
S5_NPOW = 8

F32 = jnp.float32
BF16 = jnp.bfloat16
HIGHEST = lax.Precision.HIGHEST


def _sigmoid(x):
    return 0.5 * jnp.tanh(0.5 * x) + 0.5


def _silu(x):
    hx = 0.5 * x
    return hx * jnp.tanh(hx) + hx


def _gelu_tanh(x):
    c = 0.7978845608028654
    return 0.5 * x * (1.0 + jnp.tanh(c * (x + 0.044715 * (x * x * x))))


def _softplus(x):
    return jnp.maximum(x, 0.0) + jnp.log(1.0 + jnp.exp(-jnp.abs(x)))


def _mm(a, b):
    return jnp.dot(a.astype(BF16), b.astype(BF16), preferred_element_type=F32)


def _mm_nt(a, b):
    return lax.dot_general(a.astype(BF16), b.astype(BF16), (((1,), (1,)), ((), ())),
                           preferred_element_type=F32)


def _mm_tn(a, b):
    return lax.dot_general(a.astype(BF16), b.astype(BF16), (((0,), (0,)), ((), ())),
                           preferred_element_type=F32)


def _mm_f32(a, b):
    return jnp.dot(a, b, precision=HIGHEST, preferred_element_type=F32)


def _mm_nt_f32(a, b):
    return lax.dot_general(a, b, (((1,), (1,)), ((), ())), precision=HIGHEST,
                           preferred_element_type=F32)


def _in_proj_kernel(x_ref, g_ref, wa_ref, wb_ref, cw_ref, gp_ref, pa_ref, pb_ref, halo, *, tm, tiles_per_seq):
    i = pl.program_id(0)

    @pl.when(i % tiles_per_seq == 0)
    def _():
        halo[...] = jnp.zeros_like(halo)

    x = x_ref[...]
    h = x * lax.rsqrt(jnp.mean(x * x, axis=-1, keepdims=True) + EPS) * g_ref[...]
    hb = h.astype(BF16)
    bd = jnp.dot(hb, wb_ref[:, PROJ_B_BD * LANES:], preferred_element_type=F32)
    g = -jnp.exp(gp_ref[0:1, :]) * _softplus(bd + gp_ref[1:2, :])
    sb = min(tm, GATE_BLOCK)
    row = lax.broadcasted_iota(jnp.int32, (sb, sb), 0)
    col = lax.broadcasted_iota(jnp.int32, (sb, sb), 1)
    shift = CHUNK.bit_length() - 1
    tril = (((row >> shift) == (col >> shift)) & (row >= col)).astype(F32)
    g_cum = jnp.concatenate([_mm_f32(tril, g[r0:r0 + sb, :]) for r0 in range(0, tm, sb)], axis=0)
    lane = lax.broadcasted_iota(jnp.int32, (tm, LANES), 1)
    pb_ref[:, PROJ_B_BD * LANES:] = jnp.where(lane < DECAY_LANE0, _sigmoid(bd), g_cum)

    wins = []
    for kind in range(3):
        a = kind * DN_WIDTH
        raw = jnp.dot(hb, wa_ref[:, a:a + DN_WIDTH], preferred_element_type=F32)
        wins.append(jnp.concatenate([halo[kind], raw], axis=0))
        halo[kind] = raw[tm - SUBLANES:, :]
    for a in range(3 * DN_WIDTH, PROJ_A_W, D_MODEL):
        pa_ref[:, a:a + D_MODEL] = jnp.dot(hb, wa_ref[:, a:a + D_MODEL], preferred_element_type=F32)
    for a in range(0, PROJ_B_BD * LANES, D_MODEL):
        pb_ref[:, a:a + D_MODEL] = jnp.dot(hb, wb_ref[:, a:a + D_MODEL], preferred_element_type=F32)

    scale = DN_HEAD_DIM ** -0.5
    for kind in range(3):
        a = kind * DN_WIDTH
        ws = wins[kind]
        w = cw_ref[:, a:a + DN_WIDTH]
        acc = ws[SUBLANES:] * w[CONV_K - 1:CONV_K]
        for j in range(CONV_K - 1):
            acc = acc + pltpu.roll(ws, CONV_K - 1 - j, axis=0)[SUBLANES:] * w[j:j + 1]
        y = _silu(acc)
        for hl in range(DN_HEADS):
            lo = hl * LANES
            yh = y[:, lo:lo + LANES]
            if kind == 0:
                yh = yh * (lax.rsqrt(jnp.sum(yh * yh, axis=-1, keepdims=True) + EPS) * scale)
            elif kind == 1:
                yh = yh * lax.rsqrt(jnp.sum(yh * yh, axis=-1, keepdims=True) + EPS)
            pa_ref[:, a + lo:a + lo + LANES] = yh


def _in_proj(x2, gain, w_a, w_b, conv_w8, gate_p, li, seq, tm):
    t = x2.shape[0]
    return pl.pallas_call(
        functools.partial(_in_proj_kernel, tm=tm, tiles_per_seq=seq // tm),
        grid=(t // tm,),
        in_specs=[
            pl.BlockSpec((tm, D_MODEL), lambda i: (i, 0)),
            pl.BlockSpec((None, 1, D_MODEL), lambda i: (li, 0, 0)),
            pl.BlockSpec((None, D_MODEL, PROJ_A_W), lambda i: (li, 0, 0), pipeline_mode=pl.Buffered(1)),
            pl.BlockSpec((None, D_MODEL, PROJ_B_W), lambda i: (li, 0, 0), pipeline_mode=pl.Buffered(1)),
            pl.BlockSpec((None, SUBLANES, 3 * DN_WIDTH), lambda i: (li, 0, 0)),
            pl.BlockSpec((None, SUBLANES, LANES), lambda i: (li, 0, 0)),
        ],
        out_specs=[
            pl.BlockSpec((tm, PROJ_A_W), lambda i: (i, 0)),
            pl.BlockSpec((tm, PROJ_B_W), lambda i: (i, 0)),
        ],
        out_shape=[
            jax.ShapeDtypeStruct((t, PROJ_A_W), F32),
            jax.ShapeDtypeStruct((t, PROJ_B_W), F32),
        ],
        scratch_shapes=[pltpu.VMEM((3, SUBLANES, DN_WIDTH), F32)],
        compiler_params=pltpu.CompilerParams(
            dimension_semantics=("arbitrary",), vmem_limit_bytes=VMEM_LIMIT),
        name="in_proj",
    )(x2, gain, w_a, w_b, conv_w8, gate_p)


def _neumann_inverse(l_mats, block):
    n = l_mats[0].shape[0]
    row = lax.broadcasted_iota(jnp.int32, (n, n), 0)
    col = lax.broadcasted_iota(jnp.int32, (n, n), 1)
    eye = jnp.where(row == col, 1.0, 0.0)
    ms = [-l for l in l_mats]
    ps = [eye + m for m in ms]
    for _ in range((block - 1).bit_length() - 1):
        mbs = [m.astype(BF16) for m in ms]
        ms = [jnp.dot(mb, mb, preferred_element_type=F32) for mb in mbs]
        ps = [p + _mm(p, m) for p, m in zip(ps, ms)]
    return ps


def _dn_kernel(q_ref, k_ref, v_ref, bd_ref, za_ref, ra_ref, hn_ref, o_ref, state, *, ts, bt, heads):
    hb = pl.program_id(1)
    t = pl.program_id(2)
    nc = bt // CHUNK

    @pl.when(t == 0)
    def _():
        state[...] = jnp.zeros_like(state)

    head_gain = hn_ref[0:1, :]
    lane = lax.broadcasted_iota(jnp.int32, (bt, LANES), 1)
    sub = lax.broadcasted_iota(jnp.int32, (LANES, bt), 0)
    row = lax.broadcasted_iota(jnp.int32, (bt, bt), 0)
    col = lax.broadcasted_iota(jnp.int32, (bt, bt), 1)
    shift = CHUNK.bit_length() - 1
    same = (row >> shift) == (col >> shift)
    causal = same & (row >= col)
    strict = same & (row > col)

    units = [(st, hl) for st in range(ts // bt) for hl in range(heads)]
    gates = []
    for st in range(ts // bt):
        bd = bd_ref[st * bt:(st + 1) * bt, :]
        gates.append((bd, jnp.transpose(bd)))

    qn, kn, vv, beta, g_col, decay = [], [], [], [], [], []
    for st, hl in units:
        tok = slice(st * bt, (st + 1) * bt)
        qn.append(q_ref[tok, hl * LANES:(hl + 1) * LANES])
        kn.append(k_ref[tok, hl * LANES:(hl + 1) * LANES])
        vv.append(v_ref[tok, hl * LANES:(hl + 1) * LANES])
        hg = hb * heads + hl
        bd, bd_t = gates[st]
        beta.append(jnp.sum(jnp.where(lane == hg, bd, 0.0), axis=-1, keepdims=True))
        gc = jnp.sum(jnp.where(lane == hg + DECAY_LANE0, bd, 0.0), axis=-1, keepdims=True)
        gr = jnp.sum(jnp.where(sub == hg + DECAY_LANE0, bd_t, 0.0), axis=0, keepdims=True)
        g_col.append(gc)
        decay.append(jnp.exp(jnp.where(causal, gc - gr, -jnp.inf)))

    kb = [k * b for k, b in zip(kn, beta)]
    knb = [k.astype(BF16) for k in kn]
    gram = [_mm_nt(jnp.concatenate([a, q], axis=0), b) for a, q, b in zip(kb, qn, knb)]
    l_mats = [jnp.where(strict, gm[:bt] * d, 0.0) for gm, d in zip(gram, decay)]
    a_qk = [gm[bt:] * d for gm, d in zip(gram, decay)]
    t_inv = _neumann_inverse(l_mats, CHUNK)
    e_g = [jnp.exp(g) for g in g_col]
    sol = [_mm(ti, jnp.concatenate([v * b, k * e], axis=-1))
           for ti, v, b, k, e in zip(t_inv, vv, beta, kb, e_g)]
    a_sol = [_mm(a, s) for a, s in zip(a_qk, sol)]
    q_eff = [q * e - a[:, DN_HEAD_DIM:] for q, e, a in zip(qn, e_g, a_sol)]
    g_last = [jnp.concatenate(
        [jnp.broadcast_to(g[(c + 1) * CHUNK - 1:(c + 1) * CHUNK, :], (CHUNK, 1)) for c in range(nc)], axis=0)
        for g in g_col]
    k_dec = [k * jnp.exp(gl - g) for k, gl, g in zip(kn, g_last, g_col)]
    gamma = [jnp.exp(gl) for gl in g_last]
    kt_sol = [[_mm_tn(kd[c * CHUNK:(c + 1) * CHUNK, :], s[c * CHUNK:(c + 1) * CHUNK, :]) for c in range(nc)]
              for kd, s in zip(k_dec, sol)]

    s_list = [state[hl] for hl in range(heads)]
    for st in range(ts // bt):
        for c in range(nc):
            rows = slice(c * CHUNK, (c + 1) * CHUNK)
            for hl in range(heads):
                i = st * heads + hl
                s = s_list[hl]
                on_s = _mm(jnp.concatenate([q_eff[i][rows, :], kt_sol[i][c][:, DN_HEAD_DIM:]], axis=0), s)
                o = on_s[:CHUNK] + a_sol[i][rows, :DN_HEAD_DIM]
                s_list[hl] = s * gamma[i][c * CHUNK:c * CHUNK + 1, :] - on_s[CHUNK:] + kt_sol[i][c][:, :DN_HEAD_DIM]
                o = o * lax.rsqrt(jnp.mean(o * o, axis=-1, keepdims=True) + EPS) * head_gain
                tok = slice(st * bt + c * CHUNK, st * bt + (c + 1) * CHUNK)
                cols = slice(hl * LANES, (hl + 1) * LANES)
                o_ref[tok, cols] = _sigmoid(ra_ref[tok, cols]) * (o * _silu(za_ref[tok, cols]))

    for hl in range(heads):
        state[hl] = s_list[hl]


def _deltanet(proj_a, proj_b, head_n, li, bsz, seq, ts, bt, heads):
    nt = seq // ts
    wid = heads * LANES
    nqb = DN_WIDTH // wid
    tok = lambda b, h, t: b * nt + t
    return pl.pallas_call(
        functools.partial(_dn_kernel, ts=ts, bt=bt, heads=heads),
        grid=(bsz, DN_HEADS // heads, nt),
        in_specs=[
            pl.BlockSpec((ts, wid), lambda b, h, t: (tok(b, h, t), h)),
            pl.BlockSpec((ts, wid), lambda b, h, t: (tok(b, h, t), nqb + h)),
            pl.BlockSpec((ts, wid), lambda b, h, t: (tok(b, h, t), 2 * nqb + h)),
            pl.BlockSpec((ts, LANES), lambda b, h, t: (tok(b, h, t), PROJ_B_BD)),
            pl.BlockSpec((ts, wid), lambda b, h, t: (tok(b, h, t), A_ZA * nqb + h)),
            pl.BlockSpec((ts, wid), lambda b, h, t: (tok(b, h, t), B_RA * nqb + h)),
            pl.BlockSpec((None, SUBLANES, LANES), lambda b, h, t: (li, 0, 0)),
        ],
        out_specs=pl.BlockSpec((ts, wid), lambda b, h, t: (tok(b, h, t), h)),
        out_shape=jax.ShapeDtypeStruct((bsz * seq, DN_WIDTH), F32),
        scratch_shapes=[pltpu.VMEM((heads, DN_HEAD_DIM, DN_HEAD_DIM), F32)],
        compiler_params=pltpu.CompilerParams(
            dimension_semantics=("arbitrary", "arbitrary", "arbitrary"), vmem_limit_bytes=VMEM_LIMIT),
        name="deltanet",
    )(proj_a, proj_a, proj_a, proj_b, proj_a, proj_b, head_n)


def _s5_param_kernel(are, aim, ldt, arer, aimr, ldtr, brt, bit, pwr, pwi, sqr, sqi, nsqi, bbr, bbi):
    ar = are[0]
    ai = aim[0]
    dt = jnp.exp(ldt[0])
    for k in range(S5_L + 1):
        mag = jnp.exp(ar * dt * float(k))
        ang = ai * dt * float(k)
        pwr[0, k] = mag * jnp.cos(ang)
        pwi[0, k] = mag * jnp.sin(ang)
    pr = pwr[0, S5_L]
    pi = pwi[0, S5_L]
    for j in range(S5_NPOW):
        sqr[0, j] = pr
        sqi[0, j] = pi
        nsqi[0, j] = -pi
        pr, pi = pr * pr - pi * pi, 2.0 * (pr * pi)
    ar = arer[0]
    ai = aimr[0]
    dt = jnp.exp(ldtr[0])
    mag = jnp.exp(ar * dt)
    lr = mag * jnp.cos(ai * dt)
    li = mag * jnp.sin(ai * dt)
    den = ar * ar + ai * ai
    fr = ((lr - 1.0) * ar + li * ai) / den
    fi = (li * ar - (lr - 1.0) * ai) / den
    br = brt[0]
    bi = bit[0]
    bbr[0] = fr * br - fi * bi
    bbi[0] = fr * bi + fi * br


def _s5_params(a_re, a_im, log_dt, b_re, b_im):
    nl = a_re.shape[0]
    g, n, c = SSM_GROUPS, SSM_STATE, SSM_GROUP
    ldt = jnp.broadcast_to(log_dt[..., None], (nl, g, n))
    rep = lambda a: jnp.repeat(a, c, axis=1)
    brt = jnp.swapaxes(b_re, 2, 3).reshape(nl, g * c, n)
    bit = jnp.swapaxes(b_im, 2, 3).reshape(nl, g * c, n)
    small = pl.BlockSpec((1, g, n), lambda l: (l, 0, 0))
    big = pl.BlockSpec((1, g * c, n), lambda l: (l, 0, 0))
    powr = pl.BlockSpec((1, S5_L + 1, g, n), lambda l: (l, 0, 0, 0))
    sqs = pl.BlockSpec((1, S5_NPOW, g, n), lambda l: (l, 0, 0, 0))
    pw_shape = jax.ShapeDtypeStruct((nl, S5_L + 1, g, n), F32)
    sq_shape = jax.ShapeDtypeStruct((nl, S5_NPOW, g, n), F32)
    bb_shape = jax.ShapeDtypeStruct((nl, g * c, n), F32)
    return pl.pallas_call(
        _s5_param_kernel,
        grid=(nl,),
        in_specs=[small, small, small, big, big, big, big, big],
        out_specs=[powr, powr, sqs, sqs, sqs, big, big],
        out_shape=[pw_shape, pw_shape, sq_shape, sq_shape, sq_shape, bb_shape, bb_shape],
        name="s5_params",
    )(a_re, a_im, ldt, rep(a_re), rep(a_im), rep(ldt), brt, bit)


def _s5_toeplitz_kernel(pwr, pwi, btr, bti, cr_ref, ci_ref, tw, vt):
    shape = (SSM_GROUP, SSM_STATE)
    lane = lax.broadcasted_iota(jnp.int32, (SSM_GROUP, S5_LC), 1)
    for gl in range(S5_GB):
        pr = pwr[gl]
        pi = pwi[gl]
        rows = lambda p, ms: jnp.concatenate([jnp.broadcast_to(p[m:m + 1, :], shape) for m in ms], axis=0)
        tile = lambda a: jnp.concatenate([a] * S5_L, axis=0)
        cr, ci = tile(cr_ref[gl]), tile(ci_ref[gl])
        br, bi = tile(btr[gl]), tile(bti[gl])
        lr, li = rows(pr, range(S5_L)), rows(pi, range(S5_L))
        qr = cr * lr - ci * li
        qi = cr * li + ci * lr
        kall = _mm_nt_f32(btr[gl], qr) - _mm_nt_f32(bti[gl], qi)
        for s in range(S5_L):
            off = s * SSM_GROUP
            blk = kall if s == 0 else jnp.where(lane >= off, pltpu.roll(kall, off, axis=1), 0.0)
            tw[gl, off:off + SSM_GROUP, 0:S5_LC] = blk.astype(BF16)
        lr, li = rows(pr, range(1, S5_L + 1)), rows(pi, range(1, S5_L + 1))
        vt[gl] = jnp.concatenate([cr * lr - ci * li, -(cr * li + ci * lr)],
                                 axis=1).astype(BF16)
        lr, li = rows(pr, range(S5_L - 1, -1, -1)), rows(pi, range(S5_L - 1, -1, -1))
        tw[gl, :, S5_LC:] = jnp.concatenate([lr * br - li * bi, lr * bi + li * br],
                                            axis=1).astype(BF16)


def _s5_weights(pwr, pwi, sqr, sqi, nsqi, bbr, bbi, c_re, c_im):
    nl = pwr.shape[0]
    g, n, c, ln = SSM_GROUPS, SSM_STATE, SSM_GROUP, S5_L
    by_group = lambda p: jnp.transpose(p, (0, 2, 1, 3))
    btr, bti = bbr.reshape(nl, g, c, n), bbi.reshape(nl, g, c, n)
    spec = lambda a, b: pl.BlockSpec((None, S5_GB, a, b), lambda l, q: (l, q, 0, 0))
    shp = lambda a, b: jax.ShapeDtypeStruct((nl, g, a, b), BF16)
    tw, vt = pl.pallas_call(
        _s5_toeplitz_kernel,
        grid=(nl, g // S5_GB),
        in_specs=[spec(ln + 1, n)] * 2 + [spec(c, n)] * 4,
        out_specs=[spec(S5_LC, S5_LC + 2 * n), spec(S5_LC, 2 * n)],
        out_shape=[shp(S5_LC, S5_LC + 2 * n), shp(S5_LC, 2 * n)],
        name="s5_toeplitz",
    )(by_group(pwr), by_group(pwi), btr, bti, c_re, c_im)
    pair = lambda a, b: jnp.concatenate([by_group(a), by_group(b)], axis=-1)
    sc = jnp.concatenate([pair(sqr, sqr), pair(nsqi, sqi)], axis=-1)
    return tw, vt, sc


def _granule_transpose(xs):
    xs = list(xs)
    lane = lax.broadcasted_iota(jnp.int32, xs[0].shape, 1)
    gshift = SSM_GROUP.bit_length() - 1
    for m in range(S5_GB.bit_length() - 1):
        d = SSM_GROUP << m
        upper = ((lane >> (gshift + m)) & 1) == 1
        nxt = list(xs)
        for ia in range(S5_GB):
            if (ia >> m) & 1:
                continue
            ib = ia | (1 << m)
            a, b = xs[ia], xs[ib]
            nxt[ia] = jnp.where(upper, pltpu.roll(b, d, axis=1), a)
            nxt[ib] = jnp.where(upper, b, pltpu.roll(a, LANES - d, axis=1))
        xs = nxt
    return xs


def _s5_kernel(u_ref, tw_ref, v_ref, sc_ref, d_ref, y_ref, carry, *, ts):
    t = pl.program_id(1)
    i = pl.program_id(2)
    r = ts // S5_L
    half = SSM_STATE

    @pl.when(t == 0)
    def _():
        carry[pl.ds(i * S5_GB, S5_GB)] = jnp.zeros((S5_GB, SUBLANES, LANES), F32)

    a_rows = [u_ref[pl.ds(s, r, stride=S5_L), :] for s in range(S5_L)]
    folded = [_granule_transpose(a_rows[h * S5_GB:(h + 1) * S5_GB]) for h in range(S5_L // S5_GB)]
    row = lax.broadcasted_iota(jnp.int32, (r, LANES), 0)
    row8 = lax.broadcasted_iota(jnp.int32, (SUBLANES, LANES), 0)
    groups = range(S5_GB)
    txs = [jnp.dot(jnp.concatenate([f[gl] for f in folded], axis=1).astype(BF16), tw_ref[gl],
                   preferred_element_type=F32) for gl in groups]
    scs = [sc_ref[gl] for gl in groups]
    c8s = [carry[i * S5_GB + gl] for gl in groups]
    hs_ = []
    for tx, sc, c8 in zip(txs, scs, c8s):
        x = tx[:, S5_LC:]
        inj = sc[0:1, :LANES] * c8 + sc[0:1, LANES:] * pltpu.roll(c8, half, axis=1)
        hs_.append(jnp.concatenate([x[:SUBLANES] + jnp.where(row8 == 0, inj, 0.0), x[SUBLANES:]], axis=0))
    for j in range(r.bit_length() - 1):
        sh = 1 << j
        shifted = [jnp.where(row >= sh, pltpu.roll(h, sh, axis=0), 0.0) for h in hs_]
        hs_ = [h + sc[j:j + 1, :LANES] * s + sc[j:j + 1, LANES:] * pltpu.roll(s, half, axis=1)
               for h, s, sc in zip(hs_, shifted, scs)]
    y_groups = []
    for gl, (tx, h, c8) in enumerate(zip(txs, hs_, c8s)):
        h_prev = jnp.where(row >= 1, pltpu.roll(h, 1, axis=0), jnp.broadcast_to(c8[0:1], (r, LANES)))
        y_groups.append(tx[:, :S5_LC] + _mm_nt(h_prev, v_ref[gl]))
        carry[i * S5_GB + gl] = jnp.broadcast_to(h[r - 1:r], (SUBLANES, LANES))
    d_skip = d_ref[...]
    for h in range(S5_L // S5_GB):
        unfolded = _granule_transpose([y[:, h * LANES:(h + 1) * LANES] for y in y_groups])
        for q in range(S5_GB):
            s = h * S5_GB + q
            y_ref[pl.ds(s, r, stride=S5_L), :] = unfolded[q] + d_skip * a_rows[s]


def _s5(proj, tw, v, sc, d_skip, li, bsz, seq, ts):
    nt = seq // ts
    ncb = D_MODEL // LANES
    grp = lambda a, b: pl.BlockSpec((None, S5_GB, a, b), lambda b_, t, i: (li, i, 0, 0))
    return pl.pallas_call(
        functools.partial(_s5_kernel, ts=ts),
        grid=(bsz, nt, ncb),
        in_specs=[
            pl.BlockSpec((ts, LANES), lambda b, t, i: (b * nt + t, A_U * ncb + i)),
            grp(S5_LC, S5_LC + 2 * SSM_STATE),
            grp(S5_LC, 2 * SSM_STATE),
            grp(S5_NPOW, 2 * LANES),
            pl.BlockSpec((None, 1, LANES), lambda b, t, i: (li, 0, i)),
        ],
        out_specs=pl.BlockSpec((ts, LANES), lambda b, t, i: (b * nt + t, i)),
        out_shape=jax.ShapeDtypeStruct((bsz * seq, D_MODEL), F32),
        scratch_shapes=[pltpu.VMEM((SSM_GROUPS, SUBLANES, LANES), F32)],
        compiler_params=pltpu.CompilerParams(
            dimension_semantics=("arbitrary", "arbitrary", "arbitrary"), vmem_limit_bytes=VMEM_LIMIT),
        name="s5",
    )(proj, tw, v, sc, d_skip)


def _merge_kernel(ya_ref, ys_ref, zb_ref, rb_ref, x_ref, wglu_ref, bglu_ref, wout_ref, np_ref, o_ref):
    y = _gelu_tanh(ys_ref[...])
    y = y * _sigmoid(jnp.dot(y.astype(BF16), wglu_ref[...], preferred_element_type=F32) + bglu_ref[...])
    y_b = y * _silu(zb_ref[...])
    merged = ya_ref[...] + _sigmoid(rb_ref[...]) * y_b
    out = jnp.dot(merged.astype(BF16), wout_ref[...], preferred_element_type=F32)
    out = out * lax.rsqrt(jnp.mean(out * out, axis=-1, keepdims=True) + EPS) * np_ref[...]
    o_ref[...] = x_ref[...] + out


def _merge(y_a, y_s, proj_b, x2, w_glu, b_glu, w_out, norm_post, li, tm):
    t = x2.shape[0]
    col = lambda c: pl.BlockSpec((tm, D_MODEL), lambda i: (i, c))
    vec = pl.BlockSpec((None, 1, D_MODEL), lambda i: (li, 0, 0))
    mat = pl.BlockSpec((None, D_MODEL, D_MODEL), lambda i: (li, 0, 0))
    return pl.pallas_call(
        _merge_kernel,
        grid=(t // tm,),
        in_specs=[col(0), col(0), col(B_ZB), col(B_RB), col(0), mat, vec, mat, vec],
        out_specs=col(0),
        out_shape=jax.ShapeDtypeStruct((t, D_MODEL), F32),
        compiler_params=pltpu.CompilerParams(
            dimension_semantics=("arbitrary",), vmem_limit_bytes=VMEM_LIMIT),
        name="merge",
    )(y_a, y_s, proj_b, proj_b, x2, w_glu, b_glu, w_out, norm_post)


def _prep_w_in(w_in):
    o_bd = 4 * DN_WIDTH
    o_u = o_bd + 2 * DN_HEADS
    w_a = jnp.concatenate([w_in[..., :o_bd], w_in[..., o_u:o_u + D_MODEL]], axis=-1)
    pad = jnp.zeros(w_in.shape[:-1] + (LANES - 2 * DN_HEADS,), w_in.dtype)
    w_b = jnp.concatenate([w_in[..., o_u + D_MODEL:], w_in[..., o_bd:o_u], pad], axis=-1)
    return w_a.astype(BF16), w_b.astype(BF16)


def _pad_rows(a, rows=SUBLANES):
    return jnp.pad(a, ((0, 0), (0, rows - a.shape[1]), (0, 0)))


def _trunk(x, norm_pre, w_in, conv_w, a_log, dt_bias, head_norm, ssm_a_re, ssm_a_im, ssm_log_dt,
           ssm_b_re, ssm_b_im, ssm_c_re, ssm_c_im, ssm_d, w_glu, b_glu, w_out, norm_post,
           *, tm=256, tm_merge=512, ts_dn=512, bt_dn=128, dn_heads=8, ts_s5=4096):
    bsz, seq, _ = x.shape
    depth = w_in.shape[0]
    ts_s5 = min(ts_s5, seq)
    x2 = x.reshape(bsz * seq, D_MODEL)
    w_a, w_b = _prep_w_in(w_in)
    w_glu_b = w_glu.astype(BF16)
    w_out_b = w_out.astype(BF16)
    s5_tw, s5_v, s5_sc = _s5_weights(*_s5_params(ssm_a_re, ssm_a_im, ssm_log_dt, ssm_b_re, ssm_b_im),
                                     ssm_c_re, ssm_c_im)
    at_decay = lambda p: jnp.pad(p, ((0, 0), (DECAY_LANE0, LANES - DECAY_LANE0 - DN_HEADS)))
    gate_p = _pad_rows(jnp.stack([at_decay(a_log), at_decay(dt_bias)], axis=1))
    conv_w8 = _pad_rows(conv_w)
    head_n = _pad_rows(head_norm[:, None, :])
    row = lambda p: p[:, None, :]
    for li in range(depth):
        proj_a, proj_b = _in_proj(x2, row(norm_pre), w_a, w_b, conv_w8, gate_p, li, seq, tm)
        y_a = _deltanet(proj_a, proj_b, head_n, li, bsz, seq, ts_dn, bt_dn, dn_heads)
        y_s = _s5(proj_a, s5_tw, s5_v, s5_sc, row(ssm_d), li, bsz, seq, ts_s5)
        x2 = _merge(y_a, y_s, proj_b, x2, w_glu_b, row(b_glu), w_out_b, row(norm_post), li, tm_merge)
    return x2.reshape(bsz, seq, D_MODEL)


def kernel(x, norm_pre, w_in, conv_w, a_log, dt_bias, head_norm, ssm_a_re, ssm_a_im, ssm_log_dt,
           ssm_b_re, ssm_b_im, ssm_c_re, ssm_c_im, ssm_d, w_glu, b_glu, w_out, norm_post):
    return _trunk(x, norm_pre, w_in, conv_w, a_log, dt_bias, head_norm, ssm_a_re, ssm_a_im, ssm_log_dt,
                  ssm_b_re, ssm_b_im, ssm_c_re, ssm_c_im, ssm_d, w_glu, b_glu, w_out, norm_post)
```

```python
import functools

import jax
import jax.numpy as jnp
from jax import lax
from jax.experimental import pallas as pl
from jax.experimental.pallas import tpu as pltpu

D_MODEL = 1024
DN_HEADS = 8
DN_HEAD_DIM = 128
DN_WIDTH = DN_HEADS * DN_HEAD_DIM
CONV_K = 4
CHUNK = 64
SSM_GROUP = 16
SSM_GROUPS = 64
SSM_STATE = 64
EPS = 1e-6

LANES = 128
SUBLANES = 8
V7X_VMEM_BYTES = 64 * 1024 * 1024
VMEM_LIMIT = V7X_VMEM_BYTES * 7 // 8

PROJ_A_W = 5 * D_MODEL
PROJ_B_W = 3 * D_MODEL + LANES
A_ZA, A_U = 3, 4
B_ZB, B_RA, B_RB = 0, 1, 2
PROJ_B_BD = 3 * D_MODEL // LANES
DECAY_LANE0 = DN_HEADS
GATE_BLOCK = 256

S5_L = 16
S5_LC = S5_L * SSM_GROUP
S5_GB = LANES // SSM_GROUP
S5_NPOW = 8

F32 = jnp.float32
BF16 = jnp.bfloat16
HIGHEST = lax.Precision.HIGHEST


def _sigmoid(x):
    return 0.5 * jnp.tanh(0.5 * x) + 0.5


def _silu(x):
    hx = 0.5 * x
    return hx * jnp.tanh(hx) + hx


def _gelu_tanh(x):
    c = 0.7978845608028654
    return 0.5 * x * (1.0 + jnp.tanh(c * (x + 0.044715 * (x * x * x))))


def _softplus(x):
    return jnp.maximum(x, 0.0) + jnp.log(1.0 + jnp.exp(-jnp.abs(x)))


def _mm(a, b):
    return jnp.dot(a.astype(BF16), b.astype(BF16), preferred_element_type=F32)


def _mm_nt(a, b):
    return lax.dot_general(a.astype(BF16), b.astype(BF16), (((1,), (1,)), ((), ())),
                           preferred_element_type=F32)


def _mm_tn(a, b):
    return lax.dot_general(a.astype(BF16), b.astype(BF16), (((0,), (0,)), ((), ())),
                           preferred_element_type=F32)


def _mm_f32(a, b):
    return jnp.dot(a, b, precision=HIGHEST, preferred_element_type=F32)


def _mm_nt_f32(a, b):
    return lax.dot_general(a, b, (((1,), (1,)), ((), ())), precision=HIGHEST,
                           preferred_element_type=F32)


def _in_proj_kernel(x_ref, g_ref, wa_ref, wb_ref, cw_ref, gp_ref, pa_ref, pb_ref, halo, *, tm, tiles_per_seq):
    i = pl.program_id(0)

    @pl.when(i % tiles_per_seq == 0)
    def _():
        halo[...] = jnp.zeros_like(halo)

    x = x_ref[...]
    h = x * lax.rsqrt(jnp.mean(x * x, axis=-1, keepdims=True) + EPS) * g_ref[...]
    hb = h.astype(BF16)
    bd = jnp.dot(hb, wb_ref[:, PROJ_B_BD * LANES:], preferred_element_type=F32)
    g = -jnp.exp(gp_ref[0:1, :]) * _softplus(bd + gp_ref[1:2, :])
    sb = min(tm, GATE_BLOCK)
    row = lax.broadcasted_iota(jnp.int32, (sb, sb), 0)
    col = lax.broadcasted_iota(jnp.int32, (sb, sb), 1)
    shift = CHUNK.bit_length() - 1
    tril = (((row >> shift) == (col >> shift)) & (row >= col)).astype(F32)
    g_cum = jnp.concatenate([_mm_f32(tril, g[r0:r0 + sb, :]) for r0 in range(0, tm, sb)], axis=0)
    lane = lax.broadcasted_iota(jnp.int32, (tm, LANES), 1)
    pb_ref[:, PROJ_B_BD * LANES:] = jnp.where(lane < DECAY_LANE0, _sigmoid(bd), g_cum)

    wins = []
    for kind in range(3):
        a = kind * DN_WIDTH
        raw = jnp.dot(hb, wa_ref[:, a:a + DN_WIDTH], preferred_element_type=F32)
        wins.append(jnp.concatenate([halo[kind], raw], axis=0))
        halo[kind] = raw[tm - SUBLANES:, :]
    for a in range(3 * DN_WIDTH, PROJ_A_W, D_MODEL):
        pa_ref[:, a:a + D_MODEL] = jnp.dot(hb, wa_ref[:, a:a + D_MODEL], preferred_element_type=F32)
    for a in range(0, PROJ_B_BD * LANES, D_MODEL):
        pb_ref[:, a:a + D_MODEL] = jnp.dot(hb, wb_ref[:, a:a + D_MODEL], preferred_element_type=F32)

    scale = DN_HEAD_DIM ** -0.5
    for kind in range(3):
        a = kind * DN_WIDTH
        ws = wins[kind]
        w = cw_ref[:, a:a + DN_WIDTH]
        acc = ws[SUBLANES:] * w[CONV_K - 1:CONV_K]
        for j in range(CONV_K - 1):
            acc = acc + pltpu.roll(ws, CONV_K - 1 - j, axis=0)[SUBLANES:] * w[j:j + 1]
        y = _silu(acc)
        for hl in range(DN_HEADS):
            lo = hl * LANES
            yh = y[:, lo:lo + LANES]
            if kind == 0:
                yh = yh * (lax.rsqrt(jnp.sum(yh * yh, axis=-1, keepdims=True) + EPS) * scale)
            elif kind == 1:
                yh = yh * lax.rsqrt(jnp.sum(yh * yh, axis=-1, keepdims=True) + EPS)
            pa_ref[:, a + lo:a + lo + LANES] = yh


def _in_proj(x2, gain, w_a, w_b, conv_w8, gate_p, li, seq, tm):
    t = x2.shape[0]
    return pl.pallas_call(
        functools.partial(_in_proj_kernel, tm=tm, tiles_per_seq=seq // tm),
        grid=(t // tm,),
        in_specs=[
            pl.BlockSpec((tm, D_MODEL), lambda i: (i, 0)),
            pl.BlockSpec((None, 1, D_MODEL), lambda i: (li, 0, 0)),
            pl.BlockSpec((None, D_MODEL, PROJ_A_W), lambda i: (li, 0, 0), pipeline_mode=pl.Buffered(1)),
            pl.BlockSpec((None, D_MODEL, PROJ_B_W), lambda i: (li, 0, 0), pipeline_mode=pl.Buffered(1)),
            pl.BlockSpec((None, SUBLANES, 3 * DN_WIDTH), lambda i: (li, 0, 0)),
            pl.BlockSpec((None, SUBLANES, LANES), lambda i: (li, 0, 0)),
        ],
        out_specs=[
            pl.BlockSpec((tm, PROJ_A_W), lambda i: (i, 0)),
            pl.BlockSpec((tm, PROJ_B_W), lambda i: (i, 0)),
        ],
        out_shape=[
            jax.ShapeDtypeStruct((t, PROJ_A_W), F32),
            jax.ShapeDtypeStruct((t, PROJ_B_W), F32),
        ],
        scratch_shapes=[pltpu.VMEM((3, SUBLANES, DN_WIDTH), F32)],
        compiler_params=pltpu.CompilerParams(
            dimension_semantics=("arbitrary",), vmem_limit_bytes=VMEM_LIMIT),
        name="in_proj",
    )(x2, gain, w_a, w_b, conv_w8, gate_p)


def _neumann_inverse(l_mats, block):
    n = l_mats[0].shape[0]
    row = lax.broadcasted_iota(jnp.int32, (n, n), 0)
    col = lax.broadcasted_iota(jnp.int32, (n, n), 1)
    eye = jnp.where(row == col, 1.0, 0.0)
    ms = [-l for l in l_mats]
    ps = [eye + m for m in ms]
    for _ in range((block - 1).bit_length() - 1):
        mbs = [m.astype(BF16) for m in ms]
        ms = [jnp.dot(mb, mb, preferred_element_type=F32) for mb in mbs]
        ps = [p + _mm(p, m) for p, m in zip(ps, ms)]
    return ps


def _dn_kernel(q_ref, k_ref, v_ref, bd_ref, za_ref, ra_ref, hn_ref, ys_ref, zb_ref, rb_ref, x_ref,
               wglu_ref, bglu_ref, wout_ref, np_ref, o_ref, state, ya_s, *, ts, bt, heads):
    hb = pl.program_id(1)
    t = pl.program_id(2)
    nc = bt // CHUNK

    @pl.when(t == 0)
    def _():
        state[...] = jnp.zeros_like(state)

    head_gain = hn_ref[0:1, :]
    lane = lax.broadcasted_iota(jnp.int32, (bt, LANES), 1)
    sub = lax.broadcasted_iota(jnp.int32, (LANES, bt), 0)
    row = lax.broadcasted_iota(jnp.int32, (bt, bt), 0)
    col = lax.broadcasted_iota(jnp.int32, (bt, bt), 1)
    shift = CHUNK.bit_length() - 1
    same = (row >> shift) == (col >> shift)
    causal = same & (row >= col)
    strict = same & (row > col)

    units = [(st, hl) for st in range(ts // bt) for hl in range(heads)]
    gates = []
    for st in range(ts // bt):
        bd = bd_ref[st * bt:(st + 1) * bt, :]
        gates.append((bd, jnp.transpose(bd)))

    qn, kn, vv, beta, g_col, decay = [], [], [], [], [], []
    for st, hl in units:
        tok = slice(st * bt, (st + 1) * bt)
        qn.append(q_ref[tok, hl * LANES:(hl + 1) * LANES])
        kn.append(k_ref[tok, hl * LANES:(hl + 1) * LANES])
        vv.append(v_ref[tok, hl * LANES:(hl + 1) * LANES])
        hg = hb * heads + hl
        bd, bd_t = gates[st]
        beta.append(jnp.sum(jnp.where(lane == hg, bd, 0.0), axis=-1, keepdims=True))
        gc = jnp.sum(jnp.where(lane == hg + DECAY_LANE0, bd, 0.0), axis=-1, keepdims=True)
        gr = jnp.sum(jnp.where(sub == hg + DECAY_LANE0, bd_t, 0.0), axis=0, keepdims=True)
        g_col.append(gc)
        decay.append(jnp.exp(jnp.where(causal, gc - gr, -jnp.inf)))

    kb = [k * b for k, b in zip(kn, beta)]
    knb = [k.astype(BF16) for k in kn]
    gram = [_mm_nt(jnp.concatenate([a, q], axis=0), b) for a, q, b in zip(kb, qn, knb)]
    l_mats = [jnp.where(strict, gm[:bt] * d, 0.0) for gm, d in zip(gram, decay)]
    a_qk = [gm[bt:] * d for gm, d in zip(gram, decay)]
    t_inv = _neumann_inverse(l_mats, CHUNK)
    e_g = [jnp.exp(g) for g in g_col]
    sol = [_mm(ti, jnp.concatenate([v * b, k * e], axis=-1))
           for ti, v, b, k, e in zip(t_inv, vv, beta, kb, e_g)]
    a_sol = [_mm(a, s) for a, s in zip(a_qk, sol)]
    q_eff = [q * e - a[:, DN_HEAD_DIM:] for q, e, a in zip(qn, e_g, a_sol)]
    g_last = [jnp.concatenate(
        [jnp.broadcast_to(g[(c + 1) * CHUNK - 1:(c + 1) * CHUNK, :], (CHUNK, 1)) for c in range(nc)], axis=0)
        for g in g_col]
    k_dec = [k * jnp.exp(gl - g) for k, gl, g in zip(kn, g_last, g_col)]
    gamma = [jnp.exp(gl) for gl in g_last]
    kt_sol = [[_mm_tn(kd[c * CHUNK:(c + 1) * CHUNK, :], s[c * CHUNK:(c + 1) * CHUNK, :]) for c in range(nc)]
              for kd, s in zip(k_dec, sol)]

    s_list = [state[hl] for hl in range(heads)]
    for st in range(ts // bt):
        for c in range(nc):
            rows = slice(c * CHUNK, (c + 1) * CHUNK)
            for hl in range(heads):
                i = st * heads + hl
                s = s_list[hl]
                on_s = _mm(jnp.concatenate([q_eff[i][rows, :], kt_sol[i][c][:, DN_HEAD_DIM:]], axis=0), s)
                o = on_s[:CHUNK] + a_sol[i][rows, :DN_HEAD_DIM]
                s_list[hl] = s * gamma[i][c * CHUNK:c * CHUNK + 1, :] - on_s[CHUNK:] + kt_sol[i][c][:, :DN_HEAD_DIM]
                o = o * lax.rsqrt(jnp.mean(o * o, axis=-1, keepdims=True) + EPS) * head_gain
                tok = slice(st * bt + c * CHUNK, st * bt + (c + 1) * CHUNK)
                cols = slice(hl * LANES, (hl + 1) * LANES)
                ya_s[tok, cols] = _sigmoid(ra_ref[tok, cols]) * (o * _silu(za_ref[tok, cols]))

    for hl in range(heads):
        state[hl] = s_list[hl]

    y = _gelu_tanh(ys_ref[...])
    y = y * _sigmoid(jnp.dot(y.astype(BF16), wglu_ref[...], preferred_element_type=F32) + bglu_ref[...])
    y_b = y * _silu(zb_ref[...])
    merged = ya_s[...] + _sigmoid(rb_ref[...]) * y_b
    out = jnp.dot(merged.astype(BF16), wout_ref[...], preferred_element_type=F32)
    out = out * lax.rsqrt(jnp.mean(out * out, axis=-1, keepdims=True) + EPS) * np_ref[...]
    o_ref[...] = x_ref[...] + out


def _deltanet(proj_a, proj_b, head_n, y_s, x2, w_glu, b_glu, w_out, norm_post, li, bsz, seq, ts, bt, heads):
    assert heads == DN_HEADS, "the fused merge needs full-width rows"
    nt = seq // ts
    wid = heads * LANES
    nqb = DN_WIDTH // wid
    tok = lambda b, h, t: b * nt + t
    full = lambda c: pl.BlockSpec((ts, D_MODEL), lambda b, h, t: (tok(b, h, t), c))
    vec = pl.BlockSpec((None, 1, D_MODEL), lambda b, h, t: (li, 0, 0))
    mat = pl.BlockSpec((None, D_MODEL, D_MODEL), lambda b, h, t: (li, 0, 0))
    return pl.pallas_call(
        functools.partial(_dn_kernel, ts=ts, bt=bt, heads=heads),
        grid=(bsz, DN_HEADS // heads, nt),
        in_specs=[
            pl.BlockSpec((ts, wid), lambda b, h, t: (tok(b, h, t), h)),
            pl.BlockSpec((ts, wid), lambda b, h, t: (tok(b, h, t), nqb + h)),
            pl.BlockSpec((ts, wid), lambda b, h, t: (tok(b, h, t), 2 * nqb + h)),
            pl.BlockSpec((ts, LANES), lambda b, h, t: (tok(b, h, t), PROJ_B_BD)),
            pl.BlockSpec((ts, wid), lambda b, h, t: (tok(b, h, t), A_ZA * nqb + h)),
            pl.BlockSpec((ts, wid), lambda b, h, t: (tok(b, h, t), B_RA * nqb + h)),
            pl.BlockSpec((None, SUBLANES, LANES), lambda b, h, t: (li, 0, 0)),
            full(0), full(B_ZB), full(B_RB), full(0), mat, vec, mat, vec,
        ],
        out_specs=full(0),
        out_shape=jax.ShapeDtypeStruct((bsz * seq, D_MODEL), F32),
        scratch_shapes=[pltpu.VMEM((heads, DN_HEAD_DIM, DN_HEAD_DIM), F32), pltpu.VMEM((ts, D_MODEL), F32)],
        compiler_params=pltpu.CompilerParams(
            dimension_semantics=("arbitrary", "arbitrary", "arbitrary"), vmem_limit_bytes=VMEM_LIMIT),
        name="deltanet_merge",
    )(proj_a, proj_a, proj_a, proj_b, proj_a, proj_b, head_n, y_s, proj_b, proj_b, x2,
      w_glu, b_glu, w_out, norm_post)


def _s5_param_kernel(are, aim, ldt, arer, aimr, ldtr, brt, bit, pwr, pwi, sqr, sqi, nsqi, bbr, bbi):
    ar = are[0]
    ai = aim[0]
    dt = jnp.exp(ldt[0])
    for k in range(S5_L + 1):
        mag = jnp.exp(ar * dt * float(k))
        ang = ai * dt * float(k)
        pwr[0, k] = mag * jnp.cos(ang)
        pwi[0, k] = mag * jnp.sin(ang)
    pr = pwr[0, S5_L]
    pi = pwi[0, S5_L]
    for j in range(S5_NPOW):
        sqr[0, j] = pr
        sqi[0, j] = pi
        nsqi[0, j] = -pi
        pr, pi = pr * pr - pi * pi, 2.0 * (pr * pi)
    ar = arer[0]
    ai = aimr[0]
    dt = jnp.exp(ldtr[0])
    mag = jnp.exp(ar * dt)
    lr = mag * jnp.cos(ai * dt)
    li = mag * jnp.sin(ai * dt)
    den = ar * ar + ai * ai
    fr = ((lr - 1.0) * ar + li * ai) / den
    fi = (li * ar - (lr - 1.0) * ai) / den
    br = brt[0]
    bi = bit[0]
    bbr[0] = fr * br - fi * bi
    bbi[0] = fr * bi + fi * br


def _s5_params(a_re, a_im, log_dt, b_re, b_im):
    nl = a_re.shape[0]
    g, n, c = SSM_GROUPS, SSM_STATE, SSM_GROUP
    ldt = jnp.broadcast_to(log_dt[..., None], (nl, g, n))
    rep = lambda a: jnp.repeat(a, c, axis=1)
    brt = jnp.swapaxes(b_re, 2, 3).reshape(nl, g * c, n)
    bit = jnp.swapaxes(b_im, 2, 3).reshape(nl, g * c, n)
    small = pl.BlockSpec((1, g, n), lambda l: (l, 0, 0))
    big = pl.BlockSpec((1, g * c, n), lambda l: (l, 0, 0))
    powr = pl.BlockSpec((1, S5_L + 1, g, n), lambda l: (l, 0, 0, 0))
    sqs = pl.BlockSpec((1, S5_NPOW, g, n), lambda l: (l, 0, 0, 0))
    pw_shape = jax.ShapeDtypeStruct((nl, S5_L + 1, g, n), F32)
    sq_shape = jax.ShapeDtypeStruct((nl, S5_NPOW, g, n), F32)
    bb_shape = jax.ShapeDtypeStruct((nl, g * c, n), F32)
    return pl.pallas_call(
        _s5_param_kernel,
        grid=(nl,),
        in_specs=[small, small, small, big, big, big, big, big],
        out_specs=[powr, powr, sqs, sqs, sqs, big, big],
        out_shape=[pw_shape, pw_shape, sq_shape, sq_shape, sq_shape, bb_shape, bb_shape],
        name="s5_params",
    )(a_re, a_im, ldt, rep(a_re), rep(a_im), rep(ldt), brt, bit)


def _s5_toeplitz_kernel(pwr, pwi, btr, bti, cr_ref, ci_ref, tw, vt):
    shape = (SSM_GROUP, SSM_STATE)
    lane = lax.broadcasted_iota(jnp.int32, (SSM_GROUP, S5_LC), 1)
    for gl in range(S5_GB):
        pr = pwr[gl]
        pi = pwi[gl]
        rows = lambda p, ms: jnp.concatenate([jnp.broadcast_to(p[m:m + 1, :], shape) for m in ms], axis=0)
        tile = lambda a: jnp.concatenate([a] * S5_L, axis=0)
        cr, ci = tile(cr_ref[gl]), tile(ci_ref[gl])
        br, bi = tile(btr[gl]), tile(bti[gl])
        lr, li = rows(pr, range(S5_L)), rows(pi, range(S5_L))
        qr = cr * lr - ci * li
        qi = cr * li + ci * lr
        kall = _mm_nt_f32(btr[gl], qr) - _mm_nt_f32(bti[gl], qi)
        for s in range(S5_L):
            off = s * SSM_GROUP
            blk = kall if s == 0 else jnp.where(lane >= off, pltpu.roll(kall, off, axis=1), 0.0)
            tw[gl, off:off + SSM_GROUP, 0:S5_LC] = blk.astype(BF16)
        lr, li = rows(pr, range(1, S5_L + 1)), rows(pi, range(1, S5_L + 1))
        vt[gl] = jnp.concatenate([cr * lr - ci * li, -(cr * li + ci * lr)],
                                 axis=1).astype(BF16)
        lr, li = rows(pr, range(S5_L - 1, -1, -1)), rows(pi, range(S5_L - 1, -1, -1))
        tw[gl, :, S5_LC:] = jnp.concatenate([lr * br - li * bi, lr * bi + li * br],
                                            axis=1).astype(BF16)


def _s5_weights(pwr, pwi, sqr, sqi, nsqi, bbr, bbi, c_re, c_im):
    nl = pwr.shape[0]
    g, n, c, ln = SSM_GROUPS, SSM_STATE, SSM_GROUP, S5_L
    by_group = lambda p: jnp.transpose(p, (0, 2, 1, 3))
    btr, bti = bbr.reshape(nl, g, c, n), bbi.reshape(nl, g, c, n)
    spec = lambda a, b: pl.BlockSpec((None, S5_GB, a, b), lambda l, q: (l, q, 0, 0))
    shp = lambda a, b: jax.ShapeDtypeStruct((nl, g, a, b), BF16)
    tw, vt = pl.pallas_call(
        _s5_toeplitz_kernel,
        grid=(nl, g // S5_GB),
        in_specs=[spec(ln + 1, n)] * 2 + [spec(c, n)] * 4,
        out_specs=[spec(S5_LC, S5_LC + 2 * n), spec(S5_LC, 2 * n)],
        out_shape=[shp(S5_LC, S5_LC + 2 * n), shp(S5_LC, 2 * n)],
        name="s5_toeplitz",
    )(by_group(pwr), by_group(pwi), btr, bti, c_re, c_im)
    pair = lambda a, b: jnp.concatenate([by_group(a), by_group(b)], axis=-1)
    sc = jnp.concatenate([pair(sqr, sqr), pair(nsqi, sqi)], axis=-1)
    return tw, vt, sc


def _granule_transpose(xs):
    xs = list(xs)
    lane = lax.broadcasted_iota(jnp.int32, xs[0].shape, 1)
    gshift = SSM_GROUP.bit_length() - 1
    for m in range(S5_GB.bit_length() - 1):
        d = SSM_GROUP << m
        upper = ((lane >> (gshift + m)) & 1) == 1
        nxt = list(xs)
        for ia in range(S5_GB):
            if (ia >> m) & 1:
                continue
            ib = ia | (1 << m)
            a, b = xs[ia], xs[ib]
            nxt[ia] = jnp.where(upper, pltpu.roll(b, d, axis=1), a)
            nxt[ib] = jnp.where(upper, b, pltpu.roll(a, LANES - d, axis=1))
        xs = nxt
    return xs


def _s5_kernel(u_ref, tw_ref, v_ref, sc_ref, d_ref, y_ref, carry, *, ts):
    t = pl.program_id(1)
    i = pl.program_id(2)
    r = ts // S5_L
    half = SSM_STATE

    @pl.when(t == 0)
    def _():
        carry[pl.ds(i * S5_GB, S5_GB)] = jnp.zeros((S5_GB, SUBLANES, LANES), F32)

    a_rows = [u_ref[pl.ds(s, r, stride=S5_L), :] for s in range(S5_L)]
    folded = [_granule_transpose(a_rows[h * S5_GB:(h + 1) * S5_GB]) for h in range(S5_L // S5_GB)]
    row = lax.broadcasted_iota(jnp.int32, (r, LANES), 0)
    row8 = lax.broadcasted_iota(jnp.int32, (SUBLANES, LANES), 0)
    groups = range(S5_GB)
    txs = [jnp.dot(jnp.concatenate([f[gl] for f in folded], axis=1).astype(BF16), tw_ref[gl],
                   preferred_element_type=F32) for gl in groups]
    scs = [sc_ref[gl] for gl in groups]
    c8s = [carry[i * S5_GB + gl] for gl in groups]
    hs_ = []
    for tx, sc, c8 in zip(txs, scs, c8s):
        x = tx[:, S5_LC:]
        inj = sc[0:1, :LANES] * c8 + sc[0:1, LANES:] * pltpu.roll(c8, half, axis=1)
        hs_.append(jnp.concatenate([x[:SUBLANES] + jnp.where(row8 == 0, inj, 0.0), x[SUBLANES:]], axis=0))
    for j in range(r.bit_length() - 1):
        sh = 1 << j
        shifted = [jnp.where(row >= sh, pltpu.roll(h, sh, axis=0), 0.0) for h in hs_]
        hs_ = [h + sc[j:j + 1, :LANES] * s + sc[j:j + 1, LANES:] * pltpu.roll(s, half, axis=1)
               for h, s, sc in zip(hs_, shifted, scs)]
    y_groups = []
    for gl, (tx, h, c8) in enumerate(zip(txs, hs_, c8s)):
        h_prev = jnp.where(row >= 1, pltpu.roll(h, 1, axis=0), jnp.broadcast_to(c8[0:1], (r, LANES)))
        y_groups.append(tx[:, :S5_LC] + _mm_nt(h_prev, v_ref[gl]))
        carry[i * S5_GB + gl] = jnp.broadcast_to(h[r - 1:r], (SUBLANES, LANES))
    d_skip = d_ref[...]
    for h in range(S5_L // S5_GB):
        unfolded = _granule_transpose([y[:, h * LANES:(h + 1) * LANES] for y in y_groups])
        for q in range(S5_GB):
            s = h * S5_GB + q
            y_ref[pl.ds(s, r, stride=S5_L), :] = unfolded[q] + d_skip * a_rows[s]


def _s5(proj, tw, v, sc, d_skip, li, bsz, seq, ts):
    nt = seq // ts
    ncb = D_MODEL // LANES
    grp = lambda a, b: pl.BlockSpec((None, S5_GB, a, b), lambda b_, t, i: (li, i, 0, 0))
    return pl.pallas_call(
        functools.partial(_s5_kernel, ts=ts),
        grid=(bsz, nt, ncb),
        in_specs=[
            pl.BlockSpec((ts, LANES), lambda b, t, i: (b * nt + t, A_U * ncb + i)),
            grp(S5_LC, S5_LC + 2 * SSM_STATE),
            grp(S5_LC, 2 * SSM_STATE),
            grp(S5_NPOW, 2 * LANES),
            pl.BlockSpec((None, 1, LANES), lambda b, t, i: (li, 0, i)),
        ],
        out_specs=pl.BlockSpec((ts, LANES), lambda b, t, i: (b * nt + t, i)),
        out_shape=jax.ShapeDtypeStruct((bsz * seq, D_MODEL), F32),
        scratch_shapes=[pltpu.VMEM((SSM_GROUPS, SUBLANES, LANES), F32)],
        compiler_params=pltpu.CompilerParams(
            dimension_semantics=("arbitrary", "arbitrary", "arbitrary"), vmem_limit_bytes=VMEM_LIMIT),
        name="s5",
    )(proj, tw, v, sc, d_skip)


def _merge_kernel(ya_ref, ys_ref, zb_ref, rb_ref, x_ref, wglu_ref, bglu_ref, wout_ref, np_ref, o_ref):
    y = _gelu_tanh(ys_ref[...])
    y = y * _sigmoid(jnp.dot(y.astype(BF16), wglu_ref[...], preferred_element_type=F32) + bglu_ref[...])
    y_b = y * _silu(zb_ref[...])
    merged = ya_ref[...] + _sigmoid(rb_ref[...]) * y_b
    out = jnp.dot(merged.astype(BF16), wout_ref[...], preferred_element_type=F32)
    out = out * lax.rsqrt(jnp.mean(out * out, axis=-1, keepdims=True) + EPS) * np_ref[...]
    o_ref[...] = x_ref[...] + out


def _merge(y_a, y_s, proj_b, x2, w_glu, b_glu, w_out, norm_post, li, tm):
    t = x2.shape[0]
    col = lambda c: pl.BlockSpec((tm, D_MODEL), lambda i: (i, c))
    vec = pl.BlockSpec((None, 1, D_MODEL), lambda i: (li, 0, 0))
    mat = pl.BlockSpec((None, D_MODEL, D_MODEL), lambda i: (li, 0, 0))
    return pl.pallas_call(
        _merge_kernel,
        grid=(t // tm,),
        in_specs=[col(0), col(0), col(B_ZB), col(B_RB), col(0), mat, vec, mat, vec],
        out_specs=col(0),
        out_shape=jax.ShapeDtypeStruct((t, D_MODEL), F32),
        compiler_params=pltpu.CompilerParams(
            dimension_semantics=("arbitrary",), vmem_limit_bytes=VMEM_LIMIT),
        name="merge",
    )(y_a, y_s, proj_b, proj_b, x2, w_glu, b_glu, w_out, norm_post)


def _prep_w_in(w_in):
    o_bd = 4 * DN_WIDTH
    o_u = o_bd + 2 * DN_HEADS
    w_a = jnp.concatenate([w_in[..., :o_bd], w_in[..., o_u:o_u + D_MODEL]], axis=-1)
    pad = jnp.zeros(w_in.shape[:-1] + (LANES - 2 * DN_HEADS,), w_in.dtype)
    w_b = jnp.concatenate([w_in[..., o_u + D_MODEL:], w_in[..., o_bd:o_u], pad], axis=-1)
    return w_a.astype(BF16), w_b.astype(BF16)


def _pad_rows(a, rows=SUBLANES):
    return jnp.pad(a, ((0, 0), (0, rows - a.shape[1]), (0, 0)))


def _trunk(x, norm_pre, w_in, conv_w, a_log, dt_bias, head_norm, ssm_a_re, ssm_a_im, ssm_log_dt,
           ssm_b_re, ssm_b_im, ssm_c_re, ssm_c_im, ssm_d, w_glu, b_glu, w_out, norm_post,
           *, tm=256, ts_dn=256, bt_dn=128, dn_heads=8, ts_s5=4096):
    bsz, seq, _ = x.shape
    depth = w_in.shape[0]
    ts_s5 = min(ts_s5, seq)
    x2 = x.reshape(bsz * seq, D_MODEL)
    w_a, w_b = _prep_w_in(w_in)
    w_glu_b = w_glu.astype(BF16)
    w_out_b = w_out.astype(BF16)
    s5_tw, s5_v, s5_sc = _s5_weights(*_s5_params(ssm_a_re, ssm_a_im, ssm_log_dt, ssm_b_re, ssm_b_im),
                                     ssm_c_re, ssm_c_im)
    at_decay = lambda p: jnp.pad(p, ((0, 0), (DECAY_LANE0, LANES - DECAY_LANE0 - DN_HEADS)))
    gate_p = _pad_rows(jnp.stack([at_decay(a_log), at_decay(dt_bias)], axis=1))
    conv_w8 = _pad_rows(conv_w)
    head_n = _pad_rows(head_norm[:, None, :])
    row = lambda p: p[:, None, :]
    for li in range(depth):
        proj_a, proj_b = _in_proj(x2, row(norm_pre), w_a, w_b, conv_w8, gate_p, li, seq, tm)
        y_s = _s5(proj_a, s5_tw, s5_v, s5_sc, row(ssm_d), li, bsz, seq, ts_s5)
        x2 = _deltanet(proj_a, proj_b, head_n, y_s, x2, w_glu_b, row(b_glu), w_out_b, row(norm_post),
                       li, bsz, seq, ts_dn, bt_dn, dn_heads)
    return x2.reshape(bsz, seq, D_MODEL)


def kernel(x, norm_pre, w_in, conv_w, a_log, dt_bias, head_norm, ssm_a_re, ssm_a_im, ssm_log_dt,
           ssm_b_re, ssm_b_im, ssm_c_re, ssm_c_im, ssm_d, w_glu, b_glu, w_out, norm_post):
    return _trunk(x, norm_pre, w_in, conv_w, a_log, dt_bias, head_norm, ssm_a_re, ssm_a_im, ssm_log_dt,
                  ssm_b_re, ssm_b_im, ssm_c_re, ssm_c_im, ssm_d, w_glu, b_glu, w_out, norm_post)
```
